```python
import math
import jax, jax.numpy as jnp
from jax import lax
import numpy as np

D_MODEL = 1024
BATCH = 2
SEQ = 8192
DEPTH = 2

CHUNK = 64
N_META = 16

D_MIX = D_MODEL
HEAD_DIM = 64
LRU_WIDTH = D_MIX // 4
LRU_HEADS = LRU_WIDTH // HEAD_DIM
CONV_A = 4
LRU_C = 8.0
FOX_WIDTH = D_MIX // 2
FOX_HEADS = FOX_WIDTH // HEAD_DIM
Q_BLOCK = 128
RWKV_WIDTH = D_MIX - LRU_WIDTH - FOX_WIDTH
RWKV_HEADS = RWKV_WIDTH // HEAD_DIM
W_RANK = 64
A_RANK = 64
G_RANK = 128
V_RANK = 32
GN_EPS = 64e-5
A_COLS = 2 * LRU_WIDTH
B_COLS = 3 * FOX_WIDTH + FOX_HEADS
C_COLS = 3 * RWKV_WIDTH + W_RANK + A_RANK + G_RANK
D_IN = A_COLS + B_COLS + C_COLS
C_SPLITS = (RWKV_WIDTH, 2 * RWKV_WIDTH, 3 * RWKV_WIDTH, 3 * RWKV_WIDTH + W_RANK, 3 * RWKV_WIDTH + W_RANK + A_RANK)
D_FF = 2816
CONV_F = 3
EPS = 1e-6

kernel_name = "hybrid_lru_fox_rwkv7_stream_block"


def rms_norm(x, g):
    xf = x.astype(jnp.float32)
    y = xf * lax.rsqrt(jnp.mean(xf * xf, axis=-1, keepdims=True) + EPS)
    return (y * g.astype(jnp.float32)).astype(x.dtype)


def causal_dwconv(x, w):
    k = w.shape[0]
    return lax.conv_general_dilated(
        x, w[:, None, :].astype(x.dtype), window_strides=(1,), padding=[(k - 1, 0)],
        dimension_numbers=('NWC', 'WIO', 'NWC'), feature_group_count=x.shape[-1])


def token_shift(p):
    return jnp.pad(p, ((0, 0), (1, 0), (0, 0)))[:, :-1]


def rg_lru_group(xa, ya, conv_w, conv_b, ga_w, ga_b, gx_w, gx_b, lam):
    bsz, t, _ = xa.shape
    u = causal_dwconv(xa, conv_w) + conv_b
    uh = u.reshape(bsz, t, LRU_HEADS, HEAD_DIM)
    r = jax.nn.sigmoid(jnp.einsum('bthi,hij->bthj', uh, ga_w).reshape(bsz, t, LRU_WIDTH) + ga_b)
    i = jax.nn.sigmoid(jnp.einsum('bthi,hij->bthj', uh, gx_w).reshape(bsz, t, LRU_WIDTH) + gx_b)
    log_a = -LRU_C * r.astype(jnp.float32) * jax.nn.softplus(-lam.astype(jnp.float32))
    a = jnp.exp(log_a)
    b = jnp.sqrt(-jnp.expm1(2.0 * log_a)) * (i * u).astype(jnp.float32)

    def combine(left, right):
        a1, b1 = left
        a2, b2 = right
        return a1 * a2, a2 * b1 + b2

    _, h = lax.associative_scan(combine, (a, b), axis=1)
    return h.astype(xa.dtype) * jax.nn.gelu(ya, approximate=True)


def forgetting_attention_group(q, k, v, f_logit, f_bias):
    bsz, t, _ = q.shape
    n_blk = -(-t // Q_BLOCK)
    tp = n_blk * Q_BLOCK

    def heads(z):
        z = jnp.pad(z, ((0, 0), (0, tp - t), (0, 0)))
        return z.reshape(bsz, tp, FOX_HEADS, HEAD_DIM).transpose(0, 2, 1, 3)

    qh = heads(q) * (HEAD_DIM ** -0.5)
    kh = heads(k)
    vh = heads(v)
    log_f = jax.nn.log_sigmoid((f_logit + f_bias).astype(jnp.float32))
    log_f = jnp.pad(log_f, ((0, 0), (0, tp - t), (0, 0)))
    c = jnp.cumsum(log_f, axis=1).transpose(0, 2, 1)
    kpos = jnp.arange(tp)

    def block(i):
        start = i * Q_BLOCK
        qb = lax.dynamic_slice_in_dim(qh, start, Q_BLOCK, axis=2)
        cb = lax.dynamic_slice_in_dim(c, start, Q_BLOCK, axis=2)
        qpos = start + jnp.arange(Q_BLOCK)
        s = jnp.einsum('bhqd,bhkd->bhqk', qb, kh).astype(jnp.float32) + (cb[..., :, None] - c[..., None, :])
        s = jnp.where(kpos[None, :] <= qpos[:, None], s, -jnp.inf)
        p = jax.nn.softmax(s, axis=-1)
        return jnp.einsum('bhqk,bhkd->bhqd', p.astype(vh.dtype), vh)

    o = lax.map(block, jnp.arange(n_blk))
    o = o.transpose(1, 0, 3, 2, 4).reshape(bsz, tp, FOX_WIDTH)
    return o[:, :t]


def rwkv7_group(pc, mu, w0, w_up, a0, a_up, g_up, k_k, k_a, r_k, ln_w, ln_b, v_first, v_mix):
    f32 = jnp.float32
    bsz, t, _ = pc.shape
    pc = pc + (token_shift(pc) - pc) * mu
    r, k, v, wd, ad, gd = jnp.split(pc, C_SPLITS, axis=-1)
    v_own = v
    if v_mix is not None:
        v0, v_down, v_up = v_mix
        v = v + (v_first - v) * jax.nn.sigmoid(v0 + (v @ v_down) @ v_up)
    w_log = -jax.nn.softplus(-(w0 + jnp.tanh(wd) @ w_up).astype(f32)) - 0.5
    decay = jnp.exp(-jnp.exp(w_log))
    a = jax.nn.sigmoid(a0 + ad @ a_up)
    g = jax.nn.sigmoid(gd) @ g_up

    def heads(z):
        return z.astype(f32).reshape(bsz, t, RWKV_HEADS, HEAD_DIM)

    kk = heads(k * k_k)
    kk = kk / jnp.maximum(jnp.sqrt(jnp.sum(kk * kk, axis=-1, keepdims=True)), 1e-12)
    k = k * (1.0 + (a - 1.0) * k_a)
    rh, kh, vh, ah, wh = heads(r), heads(k), heads(v), heads(a), heads(decay)
    xs = tuple(z.transpose(1, 0, 2, 3) for z in (rh, wh, kh, vh, kk, ah))

    def step(S, inp):
        r_t, w_t, k_t, v_t, kk_t, a_t = inp
        s_kk = jnp.einsum('bhij,bhj->bhi', S, kk_t)
        S = S * w_t[:, :, None, :] - s_kk[..., None] * (kk_t * a_t)[:, :, None, :] + v_t[..., None] * k_t[:, :, None, :]
        return S, jnp.einsum('bhij,bhj->bhi', S, r_t)

    s0 = jnp.zeros((bsz, RWKV_HEADS, HEAD_DIM, HEAD_DIM), f32)
    _, o = lax.scan(step, s0, xs)
    o = o.transpose(1, 0, 2, 3)
    mean = jnp.mean(o, axis=-1, keepdims=True)
    var = jnp.mean(jnp.square(o - mean), axis=-1, keepdims=True)
    o = ((o - mean) * lax.rsqrt(var + GN_EPS)).reshape(bsz, t, RWKV_WIDTH) * ln_w + ln_b
    bonus = (jnp.sum(rh * kh * r_k, axis=-1, keepdims=True) * vh).reshape(bsz, t, RWKV_WIDTH)
    out = ((o + bonus) * g.astype(f32)).astype(pc.dtype)
    return out, v_own


def setup_inputs(seed: int = 0) -> dict:
    key = jax.random.key(seed)
    ks = iter(jax.random.split(key, 40))
    f32 = jnp.float32

    def nrm(shape, scale):
        return scale * jax.random.normal(next(ks), shape, f32)

    def unif(shape, lo, hi):
        return jax.random.uniform(next(ks), shape, f32, lo, hi)

    L = DEPTH
    Lv = DEPTH - 1
    x = nrm((BATCH, SEQ, D_MODEL), 1.0)
    meta_tokens = nrm((N_META, D_MODEL), 1.0)
    norm_mix_pre = 1.0 + nrm((L, D_MODEL), 0.05)
    norm_mix_post = 1.0 + nrm((L, D_MODEL), 0.05)
    norm_ffn_pre = 1.0 + nrm((L, D_MODEL), 0.05)
    norm_ffn_post = 1.0 + nrm((L, D_MODEL), 0.05)
    w_in = nrm((L, D_MODEL, D_IN), D_MODEL ** -0.5)
    w_out = nrm((L, D_MIX, D_MODEL), D_MIX ** -0.5)
    lru_conv_w = nrm((L, CONV_A, LRU_WIDTH), CONV_A ** -0.5)
    lru_conv_b = nrm((L, LRU_WIDTH), 0.02)
    lru_gate_a_w = nrm((L, LRU_HEADS, HEAD_DIM, HEAD_DIM), HEAD_DIM ** -0.5)
    lru_gate_a_b = nrm((L, LRU_WIDTH), 0.02)
    lru_gate_x_w = nrm((L, LRU_HEADS, HEAD_DIM, HEAD_DIM), HEAD_DIM ** -0.5)
    lru_gate_x_b = nrm((L, LRU_WIDTH), 0.02)
    a_init = unif((L, LRU_WIDTH), 0.9, 0.999) ** (1.0 / LRU_C)
    lru_lambda = jnp.log(a_init) - jnp.log1p(-a_init)
    fox_f_bias = unif((L, FOX_HEADS), 1.0, 4.0)
    rwkv_mu = unif((L, C_COLS), 0.0, 1.0)
    rwkv_w0 = unif((L, RWKV_WIDTH), -6.0, -1.0)
    rwkv_w_up = nrm((L, W_RANK, RWKV_WIDTH), 0.1 * W_RANK ** -0.5)
    rwkv_a0 = nrm((L, RWKV_WIDTH), 0.1)
    rwkv_a_up = nrm((L, A_RANK, RWKV_WIDTH), 0.5 * A_RANK ** -0.5)
    rwkv_g_up = nrm((L, G_RANK, RWKV_WIDTH), G_RANK ** -0.5)
    rwkv_v0 = 1.0 + nrm((Lv, RWKV_WIDTH), 0.1)
    rwkv_v_down = nrm((Lv, RWKV_WIDTH, V_RANK), RWKV_WIDTH ** -0.5)
    rwkv_v_up = nrm((Lv, V_RANK, RWKV_WIDTH), 0.5 * V_RANK ** -0.5)
    rwkv_k_k = 0.85 + nrm((L, RWKV_WIDTH), 0.05)
    rwkv_k_a = 1.0 + nrm((L, RWKV_WIDTH), 0.05)
    rwkv_r_k = nrm((L, RWKV_HEADS, HEAD_DIM), 0.1)
    rwkv_ln_w = 1.0 + nrm((L, RWKV_WIDTH), 0.05)
    rwkv_ln_b = nrm((L, RWKV_WIDTH), 0.02)
    ffn_up = nrm((L, D_MODEL, 2 * D_FF), D_MODEL ** -0.5)
    ffn_conv = nrm((L, CONV_F, 2 * D_FF), CONV_F ** -0.5)
    ffn_down = nrm((L, D_FF, D_MODEL), D_FF ** -0.5)
    return {
        'x': x, 'meta_tokens': meta_tokens,
        'norm_mix_pre': norm_mix_pre, 'norm_mix_post': norm_mix_post,
        'norm_ffn_pre': norm_ffn_pre, 'norm_ffn_post': norm_ffn_post,
        'w_in': w_in, 'w_out': w_out,
        'lru_conv_w': lru_conv_w, 'lru_conv_b': lru_conv_b,
        'lru_gate_a_w': lru_gate_a_w, 'lru_gate_a_b': lru_gate_a_b,
        'lru_gate_x_w': lru_gate_x_w, 'lru_gate_x_b': lru_gate_x_b,
        'lru_lambda': lru_lambda, 'fox_f_bias': fox_f_bias,
        'rwkv_mu': rwkv_mu, 'rwkv_w0': rwkv_w0, 'rwkv_w_up': rwkv_w_up,
        'rwkv_a0': rwkv_a0, 'rwkv_a_up': rwkv_a_up, 'rwkv_g_up': rwkv_g_up,
        'rwkv_v0': rwkv_v0, 'rwkv_v_down': rwkv_v_down, 'rwkv_v_up': rwkv_v_up,
        'rwkv_k_k': rwkv_k_k, 'rwkv_k_a': rwkv_k_a, 'rwkv_r_k': rwkv_r_k,
        'rwkv_ln_w': rwkv_ln_w, 'rwkv_ln_b': rwkv_ln_b,
        'ffn_up': ffn_up, 'ffn_conv': ffn_conv, 'ffn_down': ffn_down,
    }


def reference(x, meta_tokens, norm_mix_pre, norm_mix_post, norm_ffn_pre, norm_ffn_post,
              w_in, w_out, lru_conv_w, lru_conv_b, lru_gate_a_w, lru_gate_a_b,
              lru_gate_x_w, lru_gate_x_b, lru_lambda, fox_f_bias,
              rwkv_mu, rwkv_w0, rwkv_w_up, rwkv_a0, rwkv_a_up, rwkv_g_up,
              rwkv_v0, rwkv_v_down, rwkv_v_up, rwkv_k_k, rwkv_k_a, rwkv_r_k,
              rwkv_ln_w, rwkv_ln_b, ffn_up, ffn_conv, ffn_down):
    bsz = x.shape[0]
    meta = jnp.broadcast_to(meta_tokens.astype(x.dtype)[None], (bsz, N_META, D_MODEL))
    h = jnp.concatenate([meta, x], axis=1)
    v_first = None
    for l in range(DEPTH):
        p = rms_norm(h, norm_mix_pre[l]) @ w_in[l]
        pa = p[..., :A_COLS]
        pb = p[..., A_COLS:A_COLS + B_COLS]
        pc = p[..., A_COLS + B_COLS:]
        xa, ya = jnp.split(pa, 2, axis=-1)
        oa = rg_lru_group(xa, ya, lru_conv_w[l], lru_conv_b[l], lru_gate_a_w[l], lru_gate_a_b[l],
                          lru_gate_x_w[l], lru_gate_x_b[l], lru_lambda[l])
        ob = forgetting_attention_group(pb[..., :FOX_WIDTH], pb[..., FOX_WIDTH:2 * FOX_WIDTH],
                                        pb[..., 2 * FOX_WIDTH:3 * FOX_WIDTH], pb[..., 3 * FOX_WIDTH:],
                                        fox_f_bias[l])
        v_mix = None if l == 0 else (rwkv_v0[l - 1], rwkv_v_down[l - 1], rwkv_v_up[l - 1])
        oc, v_own = rwkv7_group(pc, rwkv_mu[l], rwkv_w0[l], rwkv_w_up[l], rwkv_a0[l], rwkv_a_up[l],
                                rwkv_g_up[l], rwkv_k_k[l], rwkv_k_a[l], rwkv_r_k[l],
                                rwkv_ln_w[l], rwkv_ln_b[l], v_first, v_mix)
        if l == 0:
            v_first = v_own
        mixed = jnp.concatenate([oa, ob, oc], axis=-1) @ w_out[l]
        h = h + rms_norm(mixed, norm_mix_post[l])
        f = causal_dwconv(rms_norm(h, norm_ffn_pre[l]) @ ffn_up[l], ffn_conv[l])
        gate, val = jnp.split(f, 2, axis=-1)
        f = (jax.nn.gelu(gate, approximate=True) * val) @ ffn_down[l]
        h = h + rms_norm(f, norm_ffn_post[l])
    return h[:, N_META:]
```

```python
import functools

import jax
import jax.numpy as jnp
from jax import lax
from jax.experimental import pallas as pl
from jax.experimental.pallas import tpu as pltpu

F32 = jnp.float32
BF16 = jnp.bfloat16

D_MODEL = 1024
N_META = 16
HEAD_DIM = 64
LANES = 128
SUBLANES = 8
LRU_WIDTH = 256
LRU_C = 8.0
CONV_A = 4
FOX_WIDTH = 512
FOX_HEADS = 8
FOX_PAIRS = FOX_HEADS // 2
RWKV_WIDTH = 256
RWKV_PAIRS = RWKV_WIDTH // LANES
W_RANK = 64
A_RANK = 64
G_RANK = 128
V_RANK = 32
GN_EPS = 64e-5
D_FF = 2816
FF_CHUNK = 256
CONV_F = 3
EPS = 1e-6
RWKV_CHUNK = 64
INV_BLOCK = 16
VMEM_LIMIT = 56 * 1024 * 1024

COL_A = 0
COL_Q = COL_A + 2 * LRU_WIDTH
COL_K = COL_Q + FOX_HEADS * LANES
COL_V = COL_K + FOX_WIDTH
COL_F = COL_V + FOX_WIDTH
COL_C = COL_F + LANES
N_PROJ = COL_C + 4 * RWKV_WIDTH


def _time_tile(t_pad):
    for cand in (640, 512, 384, 256, 128):
        if t_pad % cand == 0:
            return cand
    raise ValueError(t_pad)


def _params(sem):
    return pltpu.CompilerParams(dimension_semantics=sem, vmem_limit_bytes=VMEM_LIMIT)


def _rms(x, g):
    return x * lax.rsqrt(jnp.mean(x * x, axis=-1, keepdims=True) + EPS) * g


def _softplus(x):
    return jnp.maximum(x, 0.0) + jnp.log1p(jnp.exp(-jnp.abs(x)))


def _sigmoid(x):
    return 1.0 / (1.0 + jnp.exp(-x))


def _gelu_tanh(x):
    return 0.5 * x * (1.0 + jnp.tanh(0.7978845608028654 * (x + 0.044715 * x * x * x)))


def _split(x):
    hi = x.astype(BF16)
    lo = (x - hi.astype(F32)).astype(BF16)
    return hi, lo


_NN = (((1,), (0,)), ((), ()))
_NT = (((1,), (1,)), ((), ()))
_TN = (((0,), (0,)), ((), ()))


def _mm3(a, b, dims=_NN):
    ah, al = a
    bh, bl = b
    d = functools.partial(lax.dot_general, dimension_numbers=dims, preferred_element_type=F32)
    return d(ah, bh) + (d(ah, bl) + d(al, bh))


def _mmf(a, b, dims=_NN):
    return _mm3(_split(a), _split(b), dims)


def _inproj_body(h_ref, g_ref, w_ref, pa_ref, q_ref, k_ref, v_ref, f_ref, pc_ref):
    xn = _rms(h_ref[...], g_ref[...]).astype(BF16)

    def proj(lo, n):
        return jnp.dot(xn, w_ref[:, lo:lo + n], preferred_element_type=F32)

    pa_ref[...] = proj(COL_A, 2 * LRU_WIDTH)
    q = proj(COL_Q, FOX_HEADS * LANES)
    for hd in range(FOX_HEADS):
        q_ref[hd] = q[:, hd * LANES:(hd + 1) * LANES].astype(BF16)
    k = proj(COL_K, FOX_WIDTH)
    v = proj(COL_V, FOX_WIDTH)
    for g in range(FOX_PAIRS):
        k_ref[g] = k[:, g * LANES:(g + 1) * LANES].astype(BF16)
        v_ref[g] = v[:, g * LANES:(g + 1) * LANES].astype(BF16)
    f_ref[...] = proj(COL_F, LANES)
    pc_ref[...] = proj(COL_C, 4 * RWKV_WIDTH)


def _inproj(h, g, w, tm):
    rows = h.shape[0]
    row = lambda i: (i, 0)
    return pl.pallas_call(
        _inproj_body,
        grid=(rows // tm,),
        in_specs=[pl.BlockSpec((tm, D_MODEL), row),
                  pl.BlockSpec((1, D_MODEL), lambda i: (0, 0)),
                  pl.BlockSpec((D_MODEL, N_PROJ), lambda i: (0, 0))],
        out_specs=[pl.BlockSpec((tm, 2 * LRU_WIDTH), row),
                   pl.BlockSpec((FOX_HEADS, tm, LANES), lambda i: (0, i, 0)),
                   pl.BlockSpec((FOX_PAIRS, tm, LANES), lambda i: (0, i, 0)),
                   pl.BlockSpec((FOX_PAIRS, tm, LANES), lambda i: (0, i, 0)),
                   pl.BlockSpec((tm, LANES), row),
                   pl.BlockSpec((tm, 4 * RWKV_WIDTH), row)],
        out_shape=[jax.ShapeDtypeStruct((rows, 2 * LRU_WIDTH), F32),
                   jax.ShapeDtypeStruct((FOX_HEADS, rows, LANES), BF16),
                   jax.ShapeDtypeStruct((FOX_PAIRS, rows, LANES), BF16),
                   jax.ShapeDtypeStruct((FOX_PAIRS, rows, LANES), BF16),
                   jax.ShapeDtypeStruct((rows, LANES), F32),
                   jax.ShapeDtypeStruct((rows, 4 * RWKV_WIDTH), F32)],
        compiler_params=_params(("arbitrary",)),
        name="inproj",
    )(h, g, w)


def _scan_steps(n):
    d = 1
    while d < n:
        yield d
        d *= 2


def _lru_body(pa_ref, f_ref, cw_ref, cb_ref, wa_ref, ba_ref, wx_ref, bx_ref, lam_ref, fb_ref,
              oa_ref, c_ref, ext_ref, hcar_ref, ccar_ref):
    tt = pa_ref.shape[0]

    @pl.when(pl.program_id(1) == 0)
    def _():
        ext_ref[0:SUBLANES, :] = jnp.zeros((SUBLANES, LRU_WIDTH), F32)
        hcar_ref[...] = jnp.zeros_like(hcar_ref)
        ccar_ref[...] = jnp.zeros_like(ccar_ref)

    xa = pa_ref[:, 0:LRU_WIDTH]
    ya = pa_ref[:, LRU_WIDTH:2 * LRU_WIDTH]
    ext_ref[SUBLANES:, :] = xa
    u = cb_ref[...]
    for k in range(CONV_A):
        off = SUBLANES - (CONV_A - 1) + k
        u = u + cw_ref[k:k + 1, :] * ext_ref[pl.ds(off, tt), :]
    ext_ref[0:SUBLANES, :] = xa[tt - SUBLANES:, :]

    ub = u.astype(BF16)
    r = _sigmoid(jnp.dot(ub, wa_ref[...], preferred_element_type=F32) + ba_ref[...])
    ig = _sigmoid(jnp.dot(ub, wx_ref[...], preferred_element_type=F32) + bx_ref[...])
    log_a = (-LRU_C) * r * _softplus(-lam_ref[...])
    a = jnp.exp(log_a)
    b = jnp.sqrt(-jnp.tanh(log_a) * (a * a + 1.0)) * (ig * u)

    row = lax.broadcasted_iota(jnp.int32, (tt, LRU_WIDTH), 0)
    for d in _scan_steps(tt):
        live = row >= d
        b = b + a * jnp.where(live, pltpu.roll(b, d, axis=0), 0.0)
        a = a * jnp.where(live, pltpu.roll(a, d, axis=0), 1.0)
    h = b + a * hcar_ref[0:1, :]
    hcar_ref[...] = jnp.broadcast_to(h[tt - 1:tt, :], hcar_ref.shape)
    oa_ref[...] = h * _gelu_tanh(ya)

    lf = -_softplus(-(f_ref[...] + fb_ref[...]))
    rowf = lax.broadcasted_iota(jnp.int32, (tt, LANES), 0)
    for d in _scan_steps(tt):
        lf = lf + jnp.where(rowf >= d, pltpu.roll(lf, d, axis=0), 0.0)
    c = lf + ccar_ref[0:1, :]
    ccar_ref[...] = jnp.broadcast_to(c[tt - 1:tt, :], ccar_ref.shape)
    c_ref[...] = c


def _lru(pa, flog, prm, bsz, tt):
    rows = pa.shape[0]
    nt = rows // bsz // tt
    row = lambda b, i: (b * nt + i, 0)
    const = lambda b, i: (0, 0)
    full = lambda a: pl.BlockSpec(a.shape, const)
    return pl.pallas_call(
        _lru_body,
        grid=(bsz, nt),
        in_specs=[pl.BlockSpec((tt, 2 * LRU_WIDTH), row), pl.BlockSpec((tt, LANES), row)]
                 + [full(a) for a in prm],
        out_specs=[pl.BlockSpec((tt, LRU_WIDTH), row), pl.BlockSpec((tt, LANES), row)],
        out_shape=[jax.ShapeDtypeStruct((rows, LRU_WIDTH), F32),
                   jax.ShapeDtypeStruct((rows, LANES), F32)],
        scratch_shapes=[pltpu.VMEM((tt + SUBLANES, LRU_WIDTH), F32),
                        pltpu.VMEM((SUBLANES, LRU_WIDTH), F32),
                        pltpu.VMEM((SUBLANES, LANES), F32)],
        compiler_params=_params(("arbitrary", "arbitrary")),
        name="lru",
    )(pa, flog, *prm)


def _fox_body(qi_ref, ki_ref, q_ref, k_ref, v_ref, cq_ref, ck_ref, o_ref,
              m_ref, l_ref, acc_ref, cqs_ref):
    g = pl.program_id(1)
    p = pl.program_id(2)
    qi = qi_ref[p]
    ki = ki_ref[p]
    tq = q_ref.shape[1]
    tk = k_ref.shape[0]

    @pl.when(ki == 0)
    def _():
        m_ref[...] = jnp.full_like(m_ref, -jnp.inf)
        l_ref[...] = jnp.zeros_like(l_ref)
        acc_ref[...] = jnp.zeros_like(acc_ref)
        lane = lax.broadcasted_iota(jnp.int32, (tq, LANES), 1)
        cq = cq_ref[...]
        for e in range(2):
            cqs_ref[e] = jnp.sum(jnp.where(lane == 2 * g + e, cq, 0.0), axis=1, keepdims=True)

    def update(masked):
        sub = lax.broadcasted_iota(jnp.int32, (SUBLANES, tk), 0)
        ck = ck_ref[...]
        if masked:
            causal = (lax.broadcasted_iota(jnp.int32, (tq, tk), 1)
                      <= lax.broadcasted_iota(jnp.int32, (tq, tk), 0))
        for e in range(2):
            ck_row = jnp.sum(jnp.where(sub == 2 * g + e, ck, 0.0), axis=0, keepdims=True)
            s = lax.dot_general(q_ref[e], k_ref[...], _NT, preferred_element_type=F32)
            z = s + (cqs_ref[e] - ck_row)
            if masked:
                z = jnp.where(causal, z, -jnp.inf)
            m_prev = m_ref[e]
            m_new = jnp.maximum(m_prev, jnp.max(z, axis=1, keepdims=True))
            alpha = jnp.exp(m_prev - m_new)
            pr = jnp.exp(z - m_new)
            l_ref[e] = alpha * l_ref[e] + jnp.sum(pr, axis=1, keepdims=True)
            acc_ref[e] = alpha * acc_ref[e] + jnp.dot(pr.astype(BF16), v_ref[...],
                                                      preferred_element_type=F32)
            m_ref[e] = m_new

    @pl.when(ki < qi)
    def _():
        update(False)

    @pl.when(ki == qi)
    def _():
        update(True)
        lane = lax.broadcasted_iota(jnp.int32, (tq, LANES), 1)
        o = jnp.where(lane < HEAD_DIM, acc_ref[0] / l_ref[0], acc_ref[1] / l_ref[1])
        o_ref[...] = o.astype(BF16)


def _fox(qe, k2, v2, c, ct, bsz, tq):
    rows = c.shape[0]
    nq = rows // bsz // tq
    pairs = [(i, j) for i in range(nq) for j in range(i + 1)]
    qi_tab = jnp.asarray([p[0] for p in pairs], jnp.int32)
    ki_tab = jnp.asarray([p[1] for p in pairs], jnp.int32)
    grid_spec = pltpu.PrefetchScalarGridSpec(
        num_scalar_prefetch=2,
        grid=(bsz, FOX_PAIRS, len(pairs)),
        in_specs=[
            pl.BlockSpec((2, tq, LANES), lambda b, g, p, qi, ki: (g, b * nq + qi[p], 0)),
            pl.BlockSpec((None, tq, LANES), lambda b, g, p, qi, ki: (g, b * nq + ki[p], 0)),
            pl.BlockSpec((None, tq, LANES), lambda b, g, p, qi, ki: (g, b * nq + ki[p], 0)),
            pl.BlockSpec((tq, LANES), lambda b, g, p, qi, ki: (b * nq + qi[p], 0)),
            pl.BlockSpec((SUBLANES, tq), lambda b, g, p, qi, ki: (0, b * nq + ki[p])),
        ],
        out_specs=pl.BlockSpec((None, tq, LANES), lambda b, g, p, qi, ki: (g, b * nq + qi[p], 0)),
        scratch_shapes=[pltpu.VMEM((2, tq, 1), F32), pltpu.VMEM((2, tq, 1), F32),
                        pltpu.VMEM((2, tq, LANES), F32), pltpu.VMEM((2, tq, 1), F32)],
    )
    return pl.pallas_call(
        _fox_body,
        grid_spec=grid_spec,
        out_shape=jax.ShapeDtypeStruct((FOX_PAIRS, rows, LANES), BF16),
        compiler_params=_params(("arbitrary", "arbitrary", "arbitrary")),
        name="fox",
    )(qi_tab, ki_tab, qe, k2, v2, c, ct)


def _head_sum(x, ones_bd):
    hi, lo = _split(x)
    d = functools.partial(jnp.dot, preferred_element_type=F32)
    return d(hi, ones_bd) + d(lo, ones_bd)


def _unit_lower_inverse(a, eye):
    n = a.shape[0]
    r = lax.broadcasted_iota(jnp.int32, (n, n), 0) // INV_BLOCK
    c = lax.broadcasted_iota(jnp.int32, (n, n), 1) // INV_BLOCK
    dg = jnp.where(r == c, a, 0.0)
    off = a - dg
    pw = _split(dg)
    acc = eye - dg
    for _ in range(3):
        sq = _mm3(pw, pw)
        pw = _split(sq)
        acc = acc + _mm3(_split(acc), pw)
    dinv = _split(acc)
    nn = _mm3(dinv, _split(off))
    ns = _split(nn)
    m1 = eye - nn
    m2 = m1 + _mm3(_split(m1), _split(_mm3(ns, ns)))
    return _mm3(_split(m2), dinv)


def _rwkv_body(*refs, has_vmix):
    it = iter(refs)
    pc_ref = next(it)
    vf_ref = next(it) if has_vmix else None
    (mu_ref, w0_ref, wup_ref, a0_ref, aup_ref, gup_ref, kk_ref, ka_ref, rk_ref,
     lnw_ref, lnb_ref) = (next(it) for _ in range(11))
    if has_vmix:
        v0_ref, vdn_ref, vup_ref = (next(it) for _ in range(3))
    oc_ref = next(it)
    vown_ref = None if has_vmix else next(it)
    (prev_ref, h_ref, kkt_ref, rt_ref, bt_ref, kt_ref, bh_ref, kh_ref, vv_ref, gc_ref,
     o_ref) = (next(it) for _ in range(11))

    tt = pc_ref.shape[0]
    nch = tt // RWKV_CHUNK
    width = RWKV_WIDTH

    @pl.when(pl.program_id(1) == 0)
    def _():
        prev_ref[...] = jnp.zeros_like(prev_ref)
        h_ref[...] = jnp.zeros_like(h_ref)

    pc = pc_ref[...]
    row = lax.broadcasted_iota(jnp.int32, pc.shape, 0)
    shifted = jnp.where(row == 0, prev_ref[SUBLANES - 1:SUBLANES, :], pltpu.roll(pc, 1, axis=0))
    prev_ref[...] = pc[tt - SUBLANES:, :]
    pc = pc + (shifted - pc) * mu_ref[...]
    r = pc[:, 0:width]
    k = pc[:, width:2 * width]
    v = pc[:, 2 * width:3 * width]
    wa_dn = pc[:, 3 * width:3 * width + LANES]
    g_dn = pc[:, 3 * width + LANES:3 * width + 2 * LANES]

    rr = lax.broadcasted_iota(jnp.int32, (width, width), 0) // HEAD_DIM
    cc = lax.broadcasted_iota(jnp.int32, (width, width), 1) // HEAD_DIM
    ones_bd = jnp.where(rr == cc, 1.0, 0.0).astype(BF16)

    if has_vmix:
        vm = _mmf(_mmf(v, vdn_ref[...]), vup_ref[...])
        v = v + (vf_ref[...] - v) * _sigmoid(v0_ref[...] + vm)
    else:
        vown_ref[...] = v
    w_log = -_softplus(-(w0_ref[...] + _mmf(jnp.tanh(wa_dn), wup_ref[...]))) - 0.5
    lw = -jnp.exp(w_log)
    a = _sigmoid(a0_ref[...] + _mmf(wa_dn, aup_ref[...]))
    gate = _mmf(_sigmoid(g_dn), gup_ref[...])
    kkr = k * kk_ref[...]
    kk = kkr / jnp.maximum(jnp.sqrt(_head_sum(kkr * kkr, ones_bd)), 1e-12)
    k = k * (1.0 + (a - 1.0) * ka_ref[...])
    b = a * kk
    bonus = _head_sum(r * k * rk_ref[...], ones_bd) * v

    cum = lw
    rowc = lax.broadcasted_iota(jnp.int32, (tt, width), 0) % RWKV_CHUNK
    for d in _scan_steps(RWKV_CHUNK):
        cum = cum + jnp.where(rowc >= d, pltpu.roll(cum, d, axis=0), 0.0)
    cum3 = cum.reshape(nch, RWKV_CHUNK, width)
    clast = jnp.broadcast_to(cum3[:, RWKV_CHUNK - 1:RWKV_CHUNK, :], cum3.shape).reshape(tt, width)
    ginv = jnp.exp(-cum)
    ghat = jnp.exp(clast - cum)
    kkt_ref[...] = kk * jnp.exp(cum - lw)
    rt_ref[...] = r * jnp.exp(cum)
    bt_ref[...] = b * ginv
    kt_ref[...] = k * ginv
    bh_ref[...] = b * ghat
    kh_ref[...] = k * ghat
    vv_ref[...] = v
    gc_ref[...] = jnp.exp(clast)

    lane = lax.broadcasted_iota(jnp.int32, (RWKV_CHUNK, LANES), 1)
    n2 = 2 * RWKV_CHUNK
    ri = lax.broadcasted_iota(jnp.int32, (n2, n2), 0)
    ci = lax.broadcasted_iota(jnp.int32, (n2, n2), 1)
    eye = jnp.where(ri == ci, 1.0, 0.0)

    def chunk(ch, carry):
        r0 = pl.multiple_of(ch * RWKV_CHUNK, RWKV_CHUNK)
        for g in range(RWKV_PAIRS):
            ls = slice(g * LANES, (g + 1) * LANES)

            def stacked(ref):
                x = ref[pl.ds(r0, RWKV_CHUNK), ls]
                return jnp.concatenate([jnp.where(lane < HEAD_DIM, x, 0.0),
                                        jnp.where(lane < HEAD_DIM, 0.0, x)], axis=0)

            kks, rs, bts, kts = (_split(stacked(x)) for x in (kkt_ref, rt_ref, bt_ref, kt_ref))
            bhs, khs, vs = (_split(stacked(x)) for x in (bh_ref, kh_ref, vv_ref))
            hmat = _split(h_ref[g])
            a_ab = jnp.where(ri > ci, _mm3(kks, bts, _NT), 0.0)
            a_ak = jnp.where(ri > ci, _mm3(kks, kts, _NT), 0.0)
            a_rb = jnp.where(ri >= ci, _mm3(rs, bts, _NT), 0.0)
            a_rk = jnp.where(ri >= ci, _mm3(rs, kts, _NT), 0.0)
            tinv = _unit_lower_inverse(a_ab, eye)
            x = _mm3(kks, hmat) + _mm3(_split(a_ak), vs)
            u = _split(-_mm3(_split(tinv), _split(x)))
            o = _mm3(rs, hmat) + _mm3(_split(a_rb), u) + _mm3(_split(a_rk), vs)
            o_ref[pl.ds(r0, RWKV_CHUNK), ls] = o[0:RWKV_CHUNK] + o[RWKV_CHUNK:]
            gl = gc_ref[pl.ds(r0, 1), ls]
            gcol = jnp.sum(jnp.where(ri == ci, jnp.broadcast_to(gl, (n2, n2)), 0.0),
                           axis=1, keepdims=True)
            h_ref[g] = h_ref[g] * gcol + _mm3(bhs, u, _TN) + _mm3(khs, vs, _TN)
        return carry

    lax.fori_loop(0, nch, chunk, 0)

    o = o_ref[...]
    mean = _head_sum(o, ones_bd) * (1.0 / HEAD_DIM)
    dev = o - mean
    var = _head_sum(dev * dev, ones_bd) * (1.0 / HEAD_DIM)
    o = dev * lax.rsqrt(var + GN_EPS) * lnw_ref[...] + lnb_ref[...]
    oc_ref[...] = (o + bonus) * gate


def _rwkv(pc, vfirst, prm, vmix_prm, bsz, tt):
    rows = pc.shape[0]
    nt = rows // bsz // tt
    row = lambda b, i: (b * nt + i, 0)
    const = lambda b, i: (0, 0)
    full = lambda a: pl.BlockSpec(a.shape, const)
    has_vmix = vfirst is not None
    inputs = [pc] + ([vfirst] if has_vmix else []) + list(prm) + (list(vmix_prm) if has_vmix else [])
    in_specs = ([pl.BlockSpec((tt, 4 * RWKV_WIDTH), row)]
                + ([pl.BlockSpec((tt, RWKV_WIDTH), row)] if has_vmix else [])
                + [full(a) for a in prm] + ([full(a) for a in vmix_prm] if has_vmix else []))
    n_out = 1 if has_vmix else 2
    tile = pltpu.VMEM((tt, RWKV_WIDTH), F32)
    outs = pl.pallas_call(
        functools.partial(_rwkv_body, has_vmix=has_vmix),
        grid=(bsz, nt),
        in_specs=in_specs,
        out_specs=[pl.BlockSpec((tt, RWKV_WIDTH), row)] * n_out,
        out_shape=[jax.ShapeDtypeStruct((rows, RWKV_WIDTH), F32)] * n_out,
        scratch_shapes=[pltpu.VMEM((SUBLANES, 4 * RWKV_WIDTH), F32),
                        pltpu.VMEM((RWKV_PAIRS, LANES, LANES), F32)] + [tile] * 9,
        compiler_params=_params(("arbitrary", "arbitrary")),
        name="rwkv_vmix" if has_vmix else "rwkv",
    )(*inputs)
    return (outs[0], None) if has_vmix else (outs[0], outs[1])


def _outproj_body(oa_ref, ob_ref, oc_ref, w_ref, g_ref, h_ref, out_ref):
    m = jnp.concatenate([oa_ref[...].astype(BF16)] + [ob_ref[g] for g in range(FOX_PAIRS)]
                        + [oc_ref[...].astype(BF16)], axis=1)
    mixed = jnp.dot(m, w_ref[...], preferred_element_type=F32)
    out_ref[...] = h_ref[...] + _rms(mixed, g_ref[...])


def _outproj(oa, ob, oc, w, g, h, tm):
    rows = h.shape[0]
    row = lambda i: (i, 0)
    const = lambda i: (0, 0)
    return pl.pallas_call(
        _outproj_body,
        grid=(rows // tm,),
        in_specs=[pl.BlockSpec((tm, LRU_WIDTH), row),
                  pl.BlockSpec((FOX_PAIRS, tm, LANES), lambda i: (0, i, 0)),
                  pl.BlockSpec((tm, RWKV_WIDTH), row),
                  pl.BlockSpec((D_MODEL, D_MODEL), const),
                  pl.BlockSpec((1, D_MODEL), const),
                  pl.BlockSpec((tm, D_MODEL), row)],
        out_specs=pl.BlockSpec((tm, D_MODEL), row),
        out_shape=jax.ShapeDtypeStruct((rows, D_MODEL), F32),
        input_output_aliases={5: 0},
        compiler_params=_params(("arbitrary",)),
        name="outproj",
    )(oa, ob, oc, w, g, h)


def _ffn_body(h_ref, gpre_ref, wg_ref, wv_ref, cg_ref, cv_ref, wd_ref, gpost_ref, out_ref,
              xn_ref, u_ref, acc_ref, *, tiles_per_seq):
    i = pl.program_id(0)
    j = pl.program_id(1)
    tm = h_ref.shape[0]
    fc = wg_ref.shape[1]

    @pl.when(j == 0)
    def _():
        @pl.when(i % tiles_per_seq == 0)
        def _():
            xn_ref[0:SUBLANES, :] = jnp.zeros((SUBLANES, D_MODEL), BF16)

        @pl.when(i % tiles_per_seq != 0)
        def _():
            xn_ref[0:SUBLANES, :] = xn_ref[tm:tm + SUBLANES, :]

        xn_ref[SUBLANES:, :] = _rms(h_ref[...], gpre_ref[...]).astype(BF16)
        acc_ref[...] = jnp.zeros_like(acc_ref)

    xn = xn_ref[...]
    u_ref[:, 0:fc] = jnp.dot(xn, wg_ref[...], preferred_element_type=F32)
    u_ref[:, fc:2 * fc] = jnp.dot(xn, wv_ref[...], preferred_element_type=F32)
    fg = jnp.zeros((tm, fc), F32)
    fv = jnp.zeros((tm, fc), F32)
    for k in range(CONV_F):
        off = SUBLANES - (CONV_F - 1) + k
        fg = fg + cg_ref[k:k + 1, :] * u_ref[pl.ds(off, tm), 0:fc]
        fv = fv + cv_ref[k:k + 1, :] * u_ref[pl.ds(off, tm), fc:2 * fc]
    act = (_gelu_tanh(fg) * fv).astype(BF16)
    acc_ref[...] += jnp.dot(act, wd_ref[...], preferred_element_type=F32)

    @pl.when(j == pl.num_programs(1) - 1)
    def _():
        out_ref[...] = h_ref[...] + _rms(acc_ref[...], gpost_ref[...])


def _ffn(h, gpre, wup, conv, wdn, gpost, bsz, tm):
    rows = h.shape[0]
    nf = D_FF // FF_CHUNK
    row = lambda i, j: (i, 0)
    const = lambda i, j: (0, 0)
    return pl.pallas_call(
        functools.partial(_ffn_body, tiles_per_seq=rows // bsz // tm),
        grid=(rows // tm, nf),
        in_specs=[pl.BlockSpec((tm, D_MODEL), row),
                  pl.BlockSpec((1, D_MODEL), const),
                  pl.BlockSpec((D_MODEL, FF_CHUNK), lambda i, j: (0, j)),
                  pl.BlockSpec((D_MODEL, FF_CHUNK), lambda i, j: (0, nf + j)),
                  pl.BlockSpec((SUBLANES, FF_CHUNK), lambda i, j: (0, j)),
                  pl.BlockSpec((SUBLANES, FF_CHUNK), lambda i, j: (0, nf + j)),
                  pl.BlockSpec((FF_CHUNK, D_MODEL), lambda i, j: (j, 0)),
                  pl.BlockSpec((1, D_MODEL), const)],
        out_specs=pl.BlockSpec((tm, D_MODEL), row),
        out_shape=jax.ShapeDtypeStruct((rows, D_MODEL), F32),
        scratch_shapes=[pltpu.VMEM((tm + SUBLANES, D_MODEL), BF16),
                        pltpu.VMEM((tm + SUBLANES, 2 * FF_CHUNK), F32),
                        pltpu.VMEM((tm, D_MODEL), F32)],
        input_output_aliases={0: 0},
        compiler_params=_params(("arbitrary", "arbitrary")),
        name="ffn",
    )(h, gpre, wup, wup, conv, conv, wdn, gpost)


def _row(x):
    return x.reshape(1, -1).astype(F32)


def _pad_rows(x, n):
    return jnp.pad(x, ((0, n - x.shape[0]), (0, 0)))


def _block_diag(w):
    heads, d, _ = w.shape
    eye = jnp.eye(heads, dtype=w.dtype)
    return (w[:, :, None, :] * eye[:, None, :, None]).reshape(heads * d, heads * d)


def _pack_w_in(w):
    wa = w[:, :2 * LRU_WIDTH]
    o = 2 * LRU_WIDTH
    wq = w[:, o:o + FOX_WIDTH] * (HEAD_DIM ** -0.5)
    wk = w[:, o + FOX_WIDTH:o + 2 * FOX_WIDTH]
    wv = w[:, o + 2 * FOX_WIDTH:o + 3 * FOX_WIDTH]
    wf = w[:, o + 3 * FOX_WIDTH:o + 3 * FOX_WIDTH + FOX_HEADS]
    wc = w[:, o + 3 * FOX_WIDTH + FOX_HEADS:]
    q5 = wq.reshape(D_MODEL, FOX_PAIRS, 2, 1, HEAD_DIM) * jnp.eye(2, dtype=w.dtype)[None, None, :, :, None]
    wqe = q5.reshape(D_MODEL, FOX_HEADS * LANES)
    wf = jnp.pad(wf, ((0, 0), (0, LANES - FOX_HEADS)))
    return jnp.concatenate([wa, wqe, wk, wv, wf, wc], axis=1).astype(BF16)


def kernel(x, meta_tokens, norm_mix_pre, norm_mix_post, norm_ffn_pre, norm_ffn_post, w_in, w_out, lru_conv_w, lru_conv_b, lru_gate_a_w, lru_gate_a_b, lru_gate_x_w, lru_gate_x_b, lru_lambda, fox_f_bias, rwkv_mu, rwkv_w0, rwkv_w_up, rwkv_a0, rwkv_a_up, rwkv_g_up, rwkv_v0, rwkv_v_down, rwkv_v_up, rwkv_k_k, rwkv_k_a, rwkv_r_k, rwkv_ln_w, rwkv_ln_b, ffn_up, ffn_conv, ffn_down):
    bsz, seq, _ = x.shape
    depth = w_in.shape[0]
    t = N_META + seq
    t_pad = -(-t // LANES) * LANES
    tile = _time_tile(t_pad)
    rows = bsz * t_pad

    meta = jnp.broadcast_to(meta_tokens.astype(x.dtype)[None], (bsz, N_META, D_MODEL))
    h = jnp.concatenate([meta, x, jnp.zeros((bsz, t_pad - t, D_MODEL), x.dtype)], axis=1)
    h = h.reshape(rows, D_MODEL)

    v_first = None
    for l in range(depth):
        pa, qe, k2, v2, flog, pc = _inproj(h, _row(norm_mix_pre[l]), _pack_w_in(w_in[l]), tile)

        lru_prm = (_pad_rows(lru_conv_w[l], SUBLANES), _row(lru_conv_b[l]),
                   _block_diag(lru_gate_a_w[l]).astype(BF16), _row(lru_gate_a_b[l]),
                   _block_diag(lru_gate_x_w[l]).astype(BF16), _row(lru_gate_x_b[l]),
                   _row(lru_lambda[l]),
                   jnp.pad(_row(fox_f_bias[l]), ((0, 0), (0, LANES - FOX_HEADS))))
        oa, c = _lru(pa, flog, lru_prm, bsz, tile)

        ct = c[:, :SUBLANES].T
        ob = _fox(qe, k2, v2, c, ct, bsz, tile)

        rwkv_prm = (_row(rwkv_mu[l]), _row(rwkv_w0[l]),
                    _pad_rows(rwkv_w_up[l], LANES), _row(rwkv_a0[l]),
                    jnp.pad(rwkv_a_up[l], ((W_RANK, 0), (0, 0))), rwkv_g_up[l],
                    _row(rwkv_k_k[l]), _row(rwkv_k_a[l]), _row(rwkv_r_k[l]),
                    _row(rwkv_ln_w[l]), _row(rwkv_ln_b[l]))
        vmix_prm = None
        if l > 0:
            vmix_prm = (_row(rwkv_v0[l - 1]),
                        jnp.pad(rwkv_v_down[l - 1], ((0, 0), (0, LANES - V_RANK))),
                        _pad_rows(rwkv_v_up[l - 1], LANES))
        oc, v_own = _rwkv(pc, v_first if l > 0 else None, rwkv_prm, vmix_prm, bsz, tile)
        if l == 0:
            v_first = v_own

        h = _outproj(oa, ob, oc, w_out[l].astype(BF16), _row(norm_mix_post[l]), h, tile)
        h = _ffn(h, _row(norm_ffn_pre[l]), ffn_up[l].astype(BF16),
                 _pad_rows(ffn_conv[l], SUBLANES), ffn_down[l].astype(BF16),
                 _row(norm_ffn_post[l]), bsz, tile)

    return h.reshape(bsz, t_pad, D_MODEL)[:, N_META:N_META + seq]
```

```python
import functools

import jax
import jax.numpy as jnp
import numpy as np
from jax import lax
from jax.experimental import pallas as pl
from jax.experimental.pallas import tpu as pltpu

F32 = jnp.float32
BF16 = jnp.bfloat16

D_MODEL = 1024
N_META = 16
HEAD_DIM = 64
LANES = 128
SUBLANES = 8
LRU_WIDTH = 256
LRU_C = 8.0
CONV_A = 4
FOX_WIDTH = 512
FOX_HEADS = 8
FOX_PAIRS = FOX_HEADS // 2
FOX_STRIP = 32
RWKV_WIDTH = 256
RWKV_PAIRS = RWKV_WIDTH // LANES
W_RANK = 64
A_RANK = 64
G_RANK = 128
V_RANK = 32
GN_EPS = 64e-5
D_FF = 2816
FF_CHUNK = 256
CONV_F = 3
EPS = 1e-6
RWKV_CHUNK = 64
INV_BLOCK = 16
PREP_GROUP = 5
VMEM_LIMIT = 56 * 1024 * 1024

COL_A = 0
COL_Q = COL_A + 2 * LRU_WIDTH
COL_K = COL_Q + FOX_WIDTH
COL_V = COL_K + FOX_WIDTH
COL_F = COL_V + FOX_WIDTH
COL_C = COL_F + LANES
N_PROJ = COL_C + 4 * RWKV_WIDTH
BIAS_PIECES = 3
LANE_ONE = HEAD_DIM


def _time_tile(t_pad):
    for cand in (640, 512, 384, 256, 128):
        if t_pad % cand == 0:
            return cand
    raise ValueError(t_pad)


def _params(sem):
    return pltpu.CompilerParams(dimension_semantics=sem, vmem_limit_bytes=VMEM_LIMIT)


def _rms(x, g):
    return x * lax.rsqrt(jnp.mean(x * x, axis=-1, keepdims=True) + EPS) * g


def _softplus(x):
    return jnp.maximum(x, 0.0) + jnp.log1p(jnp.exp(-jnp.abs(x)))


def _sigmoid(x):
    return 1.0 / (1.0 + jnp.exp(-x))


def _gelu_tanh(x):
    return 0.5 * x * (1.0 + jnp.tanh(0.7978845608028654 * (x + 0.044715 * x * x * x)))


def _split(x):
    hi = x.astype(BF16)
    lo = (x - hi.astype(F32)).astype(BF16)
    return hi, lo


_NN = (((1,), (0,)), ((), ()))
_NT = (((1,), (1,)), ((), ()))
_TN = (((0,), (0,)), ((), ()))


def _mm3(a, b, dims=_NN):
    ah, al = a
    bh, bl = b
    d = functools.partial(lax.dot_general, dimension_numbers=dims, preferred_element_type=F32)
    return d(ah, bh) + (d(ah, bl) + d(al, bh))


def _mmf(a, b, dims=_NN):
    return _mm3(_split(a), _split(b), dims)


def _dot(a, b, dims=_NN):
    return lax.dot_general(a, b, dimension_numbers=dims, preferred_element_type=F32)


def _scan_steps(n):
    d = 1
    while d < n:
        yield d
        d *= 2


def _inproj_body(h_ref, g_ref, w_ref, fb_ref, place_ref, ones_ref,
                 pa_ref, q_ref, k_ref, v_ref, pc_ref, ccar_ref, *, tiles_per_seq):
    tm = h_ref.shape[0]
    xn = _rms(h_ref[...], g_ref[...]).astype(BF16)

    def proj(lo, n):
        return jnp.dot(xn, w_ref[:, lo:lo + n], preferred_element_type=F32)

    pa_ref[...] = proj(COL_A, 2 * LRU_WIDTH)
    pc_ref[...] = proj(COL_C, 4 * RWKV_WIDTH)

    @pl.when(pl.program_id(0) % tiles_per_seq == 0)
    def _():
        ccar_ref[...] = jnp.zeros_like(ccar_ref)

    c = -_softplus(-(proj(COL_F, LANES) + fb_ref[...]))
    row = lax.broadcasted_iota(jnp.int32, (tm, LANES), 0)
    for d in _scan_steps(tm):
        c = c + jnp.where(row >= d, pltpu.roll(c, d, axis=0), 0.0)
    c = c + ccar_ref[0:1, :]
    ccar_ref[...] = jnp.broadcast_to(c[tm - 1:tm, :], ccar_ref.shape)

    pieces = []
    rest = c
    for _ in range(BIAS_PIECES):
        piece = rest.astype(BF16)
        pieces.append(piece)
        rest = rest - piece.astype(F32)
    extra = jnp.dot(jnp.concatenate(pieces, axis=1), place_ref[...],
                    preferred_element_type=F32) + ones_ref[...]

    lane = lax.broadcasted_iota(jnp.int32, (tm, LANES), 1)
    q = proj(COL_Q, FOX_WIDTH)
    k = proj(COL_K, FOX_WIDTH)
    v = proj(COL_V, FOX_WIDTH)
    v_extra = jnp.where(lane == LANE_ONE, 1.0, 0.0)
    for hd in range(FOX_HEADS):
        g = hd // 2

        def head_tile(x):
            x = x[:, g * LANES:(g + 1) * LANES]
            return pltpu.roll(x, HEAD_DIM, axis=1) if hd % 2 else x

        qx = extra[:, hd * LANES:(hd + 1) * LANES]
        kx = extra[:, (FOX_HEADS + hd) * LANES:(FOX_HEADS + hd + 1) * LANES]
        q_ref[hd] = jnp.where(lane < HEAD_DIM, head_tile(q), qx).astype(BF16)
        k_ref[hd] = jnp.where(lane < HEAD_DIM, head_tile(k), kx).astype(BF16)
        v_ref[hd] = jnp.where(lane < HEAD_DIM, head_tile(v), v_extra).astype(BF16)


def _inproj(h, g, w, fb, place, ones, bsz, tm):
    rows = h.shape[0]
    row = lambda i: (i, 0)
    const = lambda i: (0, 0)
    head = pl.BlockSpec((FOX_HEADS, tm, LANES), lambda i: (0, i, 0))
    head_shape = jax.ShapeDtypeStruct((FOX_HEADS, rows, LANES), BF16)
    return pl.pallas_call(
        functools.partial(_inproj_body, tiles_per_seq=rows // bsz // tm),
        grid=(rows // tm,),
        in_specs=[pl.BlockSpec((tm, D_MODEL), row),
                  pl.BlockSpec((1, D_MODEL), const),
                  pl.BlockSpec((D_MODEL, N_PROJ), const),
                  pl.BlockSpec(fb.shape, const),
                  pl.BlockSpec(place.shape, const),
                  pl.BlockSpec(ones.shape, const)],
        out_specs=[pl.BlockSpec((tm, 2 * LRU_WIDTH), row), head, head, head,
                   pl.BlockSpec((tm, 4 * RWKV_WIDTH), row)],
        out_shape=[jax.ShapeDtypeStruct((rows, 2 * LRU_WIDTH), F32),
                   head_shape, head_shape, head_shape,
                   jax.ShapeDtypeStruct((rows, 4 * RWKV_WIDTH), F32)],
        scratch_shapes=[pltpu.VMEM((SUBLANES, LANES), F32)],
        compiler_params=_params(("arbitrary",)),
        name="inproj",
    )(h, g, w, fb, place, ones)


def _lru_body(pa_ref, cw_ref, cb_ref, wa_ref, ba_ref, wx_ref, bx_ref, lam_ref,
              oa_ref, ext_ref, hcar_ref):
    tt = pa_ref.shape[0]

    @pl.when(pl.program_id(1) == 0)
    def _():
        ext_ref[0:SUBLANES, :] = jnp.zeros((SUBLANES, LRU_WIDTH), F32)
        hcar_ref[...] = jnp.zeros_like(hcar_ref)

    xa = pa_ref[:, 0:LRU_WIDTH]
    ya = pa_ref[:, LRU_WIDTH:2 * LRU_WIDTH]
    ext_ref[SUBLANES:, :] = xa
    u = cb_ref[...]
    for k in range(CONV_A):
        off = SUBLANES - (CONV_A - 1) + k
        u = u + cw_ref[k:k + 1, :] * ext_ref[pl.ds(off, tt), :]
    ext_ref[0:SUBLANES, :] = xa[tt - SUBLANES:, :]

    ub = u.astype(BF16)
    r = _sigmoid(jnp.dot(ub, wa_ref[...], preferred_element_type=F32) + ba_ref[...])
    ig = _sigmoid(jnp.dot(ub, wx_ref[...], preferred_element_type=F32) + bx_ref[...])
    log_a = (-LRU_C) * r * _softplus(-lam_ref[...])
    a = jnp.exp(log_a)
    b = jnp.sqrt(-jnp.tanh(log_a) * (a * a + 1.0)) * (ig * u)

    row = lax.broadcasted_iota(jnp.int32, (tt, LRU_WIDTH), 0)
    for d in _scan_steps(tt):
        live = row >= d
        b = b + a * jnp.where(live, pltpu.roll(b, d, axis=0), 0.0)
        a = a * jnp.where(live, pltpu.roll(a, d, axis=0), 1.0)
    h = b + a * hcar_ref[0:1, :]
    hcar_ref[...] = jnp.broadcast_to(h[tt - 1:tt, :], hcar_ref.shape)
    oa_ref[...] = h * _gelu_tanh(ya)


def _lru(pa, prm, bsz, tt):
    rows = pa.shape[0]
    nt = rows // bsz // tt
    row = lambda b, i: (b * nt + i, 0)
    const = lambda b, i: (0, 0)
    full = lambda a: pl.BlockSpec(a.shape, const)
    return pl.pallas_call(
        _lru_body,
        grid=(bsz, nt),
        in_specs=[pl.BlockSpec((tt, 2 * LRU_WIDTH), row)] + [full(a) for a in prm],
        out_specs=pl.BlockSpec((tt, LRU_WIDTH), row),
        out_shape=jax.ShapeDtypeStruct((rows, LRU_WIDTH), F32),
        scratch_shapes=[pltpu.VMEM((tt + SUBLANES, LRU_WIDTH), F32),
                        pltpu.VMEM((SUBLANES, LRU_WIDTH), F32)],
        compiler_params=_params(("arbitrary", "arbitrary")),
        name="lru",
    )(pa, *prm)


def _fox_body(qi_ref, ki_ref, q_ref, k_ref, v_ref, o_ref,
              m_ref, acc_ref, mask_ref, s_ref, p_ref, mb_ref):
    p = pl.program_id(2)
    qi = qi_ref[p]
    ki = ki_ref[p]
    tq = q_ref.shape[1]
    tk = k_ref.shape[1]

    @pl.when((pl.program_id(0) == 0) & (pl.program_id(1) == 0) & (p == 0))
    def _():
        causal = (lax.broadcasted_iota(jnp.int32, (tq, tk), 1)
                  <= lax.broadcasted_iota(jnp.int32, (tq, tk), 0))
        mask_ref[...] = jnp.where(causal, 0.0, -jnp.inf)

    @pl.when(ki == 0)
    def _():
        m_ref[...] = jnp.full_like(m_ref, -jnp.inf)
        acc_ref[...] = jnp.zeros_like(acc_ref)

    def update(masked):
        for e in range(2):
            s = lax.dot_general(q_ref[e], k_ref[e], _NT, preferred_element_type=F32)
            s_ref[e] = s + mask_ref[...] if masked else s
        for e in range(2):
            for r0 in range(0, tq, FOX_STRIP):
                rows = slice(r0, r0 + FOX_STRIP)
                mx = s_ref[e, rows, 0:LANES]
                for c0 in range(LANES, tk, LANES):
                    mx = jnp.maximum(mx, s_ref[e, rows, c0:c0 + LANES])
                mb_ref[e, rows, :] = mx
            m_prev = m_ref[e]
            m_new = jnp.maximum(m_prev, jnp.max(mb_ref[e], axis=1, keepdims=True))
            m_ref[e] = m_new
            mb_ref[e] = jnp.broadcast_to(m_new, (tq, LANES))
            acc_ref[e] = jnp.exp(m_prev - m_new) * acc_ref[e]
            for r0 in range(0, tq, FOX_STRIP):
                rows = slice(r0, r0 + FOX_STRIP)
                mb = mb_ref[e, rows, :]
                for c0 in range(0, tk, LANES):
                    cols = slice(c0, c0 + LANES)
                    p_ref[e, rows, cols] = jnp.exp(s_ref[e, rows, cols] - mb).astype(BF16)
            acc_ref[e] += jnp.dot(p_ref[e], v_ref[e], preferred_element_type=F32)

    @pl.when(ki < qi)
    def _():
        update(False)

    @pl.when(ki == qi)
    def _():
        update(True)
        lane = lax.broadcasted_iota(jnp.int32, (tq, LANES), 1)
        o0, o1 = (acc_ref[e] / acc_ref[e][:, LANE_ONE:LANE_ONE + 1] for e in range(2))
        o_ref[...] = jnp.where(lane < HEAD_DIM, o0, pltpu.roll(o1, HEAD_DIM, axis=1)).astype(BF16)


def _fox(qe, ke, ve, bsz, tq):
    rows = qe.shape[1]
    nq = rows // bsz // tq
    pairs = [(i, j) for i in range(nq) for j in range(i + 1)]
    qi_tab = jnp.asarray([p[0] for p in pairs], jnp.int32)
    ki_tab = jnp.asarray([p[1] for p in pairs], jnp.int32)
    grid_spec = pltpu.PrefetchScalarGridSpec(
        num_scalar_prefetch=2,
        grid=(bsz, FOX_PAIRS, len(pairs)),
        in_specs=[
            pl.BlockSpec((2, tq, LANES), lambda b, g, p, qi, ki: (g, b * nq + qi[p], 0)),
            pl.BlockSpec((2, tq, LANES), lambda b, g, p, qi, ki: (g, b * nq + ki[p], 0)),
            pl.BlockSpec((2, tq, LANES), lambda b, g, p, qi, ki: (g, b * nq + ki[p], 0)),
        ],
        out_specs=pl.BlockSpec((None, tq, LANES), lambda b, g, p, qi, ki: (g, b * nq + qi[p], 0)),
        scratch_shapes=[pltpu.VMEM((2, tq, 1), F32), pltpu.VMEM((2, tq, LANES), F32),
                        pltpu.VMEM((tq, tq), F32), pltpu.VMEM((2, tq, tq), F32),
                        pltpu.VMEM((2, tq, tq), BF16), pltpu.VMEM((2, tq, LANES), F32)],
    )
    return pl.pallas_call(
        _fox_body,
        grid_spec=grid_spec,
        out_shape=jax.ShapeDtypeStruct((FOX_PAIRS, rows, LANES), BF16),
        compiler_params=_params(("arbitrary", "arbitrary", "arbitrary")),
        name="fox",
    )(qi_tab, ki_tab, qe, ke, ve)


def _head_sum(x, ones_bd):
    hi, lo = _split(x)
    d = functools.partial(jnp.dot, preferred_element_type=F32)
    return d(hi, ones_bd) + d(lo, ones_bd)


def _unit_lower_inverses(mats, eye):
    n = eye.shape[0]
    r = lax.broadcasted_iota(jnp.int32, (n, n), 0) // INV_BLOCK
    c = lax.broadcasted_iota(jnp.int32, (n, n), 1) // INV_BLOCK
    dgs = [jnp.where(r == c, a, 0.0) for a in mats]
    offs = [(a - dg).astype(BF16) for a, dg in zip(mats, dgs)]
    pws = [dg.astype(BF16) for dg in dgs]
    accs = [eye - dg for dg in dgs]
    for _ in range(3):
        pws = [_dot(pw, pw).astype(BF16) for pw in pws]
        accs = [acc + _dot(acc.astype(BF16), pw) for acc, pw in zip(accs, pws)]
    dinvs = [acc.astype(BF16) for acc in accs]
    nns = [_dot(dinv, off) for dinv, off in zip(dinvs, offs)]
    sqs = [_dot(nn.astype(BF16), nn.astype(BF16)).astype(BF16) for nn in nns]
    m2s = [(eye - nn) + _dot((eye - nn).astype(BF16), sq) for nn, sq in zip(nns, sqs)]
    return [_dot(m2.astype(BF16), dinv) for m2, dinv in zip(m2s, dinvs)]


def _rwkv_body(*refs, has_vmix):
    it = iter(refs)
    pc_ref = next(it)
    vf_ref = next(it) if has_vmix else None
    (mu_ref, w0_ref, wup_ref, a0_ref, aup_ref, gup_ref, kk_ref, ka_ref, rk_ref,
     lnw_ref, lnb_ref) = (next(it) for _ in range(11))
    if has_vmix:
        v0_ref, vdn_ref, vup_ref = (next(it) for _ in range(3))
    oc_ref = next(it)
    vown_ref = None if has_vmix else next(it)
    prev_ref, h_ref, gc_ref, o_ref = (next(it) for _ in range(4))
    kks_ref, rs_ref, bts_ref, kts_ref, bhs_ref, khs_ref, vs_ref = (next(it) for _ in range(7))
    tinv_ref, arb_ref, avx_ref, akv_ref, khv_ref, gful_ref = (next(it) for _ in range(6))

    tt = pc_ref.shape[0]
    nch = tt // RWKV_CHUNK
    width = RWKV_WIDTH

    @pl.when(pl.program_id(1) == 0)
    def _():
        prev_ref[...] = jnp.zeros_like(prev_ref)
        h_ref[...] = jnp.zeros_like(h_ref)

    pc = pc_ref[...]
    row = lax.broadcasted_iota(jnp.int32, pc.shape, 0)
    shifted = jnp.where(row == 0, prev_ref[SUBLANES - 1:SUBLANES, :], pltpu.roll(pc, 1, axis=0))
    prev_ref[...] = pc[tt - SUBLANES:, :]
    pc = pc + (shifted - pc) * mu_ref[...]
    r = pc[:, 0:width]
    k = pc[:, width:2 * width]
    v = pc[:, 2 * width:3 * width]
    wa_dn = pc[:, 3 * width:3 * width + LANES]
    g_dn = pc[:, 3 * width + LANES:3 * width + 2 * LANES]

    rr = lax.broadcasted_iota(jnp.int32, (width, width), 0) // HEAD_DIM
    cc = lax.broadcasted_iota(jnp.int32, (width, width), 1) // HEAD_DIM
    ones_bd = jnp.where(rr == cc, 1.0, 0.0).astype(BF16)

    if has_vmix:
        vm = _mmf(_mmf(v, vdn_ref[...]), vup_ref[...])
        v = v + (vf_ref[...] - v) * _sigmoid(v0_ref[...] + vm)
    else:
        vown_ref[...] = v
    w_log = -_softplus(-(w0_ref[...] + _mmf(jnp.tanh(wa_dn), wup_ref[...]))) - 0.5
    lw = -jnp.exp(w_log)
    a = _sigmoid(a0_ref[...] + _mmf(wa_dn, aup_ref[...]))
    gate = _mmf(_sigmoid(g_dn), gup_ref[...])
    kkr = k * kk_ref[...]
    kk = kkr / jnp.maximum(jnp.sqrt(_head_sum(kkr * kkr, ones_bd)), 1e-12)
    k = k * (1.0 + (a - 1.0) * ka_ref[...])
    b = a * kk
    bonus = _head_sum(r * k * rk_ref[...], ones_bd) * v

    cum = lw
    rowc = lax.broadcasted_iota(jnp.int32, (tt, width), 0) % RWKV_CHUNK
    for d in _scan_steps(RWKV_CHUNK):
        cum = cum + jnp.where(rowc >= d, pltpu.roll(cum, d, axis=0), 0.0)
    cum3 = cum.reshape(nch, RWKV_CHUNK, width)
    clast = jnp.broadcast_to(cum3[:, RWKV_CHUNK - 1:RWKV_CHUNK, :], cum3.shape).reshape(tt, width)
    ginv = jnp.exp(-cum)
    ghat = jnp.exp(clast - cum)
    gc_ref[...] = jnp.exp(clast)

    n2 = 2 * RWKV_CHUNK
    lane3 = lax.broadcasted_iota(jnp.int32, (nch, RWKV_CHUNK, LANES), 2)

    def stack(ref, x):
        for g in range(RWKV_PAIRS):
            xp = x[:, g * LANES:(g + 1) * LANES].reshape(nch, RWKV_CHUNK, LANES)
            st = jnp.concatenate([jnp.where(lane3 < HEAD_DIM, xp, 0.0),
                                  jnp.where(lane3 < HEAD_DIM, 0.0, xp)], axis=1)
            ref[g] = st.reshape(nch * n2, LANES).astype(BF16)

    stack(kks_ref, kk * jnp.exp(cum - lw))
    stack(rs_ref, r * jnp.exp(cum))
    stack(bts_ref, b * ginv)
    stack(kts_ref, k * ginv)
    stack(bhs_ref, b * ghat)
    stack(khs_ref, k * ghat)
    stack(vs_ref, v)

    ri = lax.broadcasted_iota(jnp.int32, (n2, n2), 0)
    ci = lax.broadcasted_iota(jnp.int32, (n2, n2), 1)
    eye = jnp.where(ri == ci, 1.0, 0.0)

    group = max(d for d in range(1, PREP_GROUP + 1) if nch % d == 0)
    pairs = range(RWKV_PAIRS)

    def prepare(it, carry):
        probs = [(g, pl.ds(pl.multiple_of((it * group + j) * n2, n2), n2),
                  pl.multiple_of((it * group + j) * RWKV_CHUNK, RWKV_CHUNK))
                 for j in range(group) for g in pairs]
        kks = [kks_ref[g, rows, :] for g, rows, _ in probs]
        rs = [rs_ref[g, rows, :] for g, rows, _ in probs]
        bts = [bts_ref[g, rows, :] for g, rows, _ in probs]
        kts = [kts_ref[g, rows, :] for g, rows, _ in probs]
        a_ab = [jnp.where(ri > ci, _dot(x, y, _NT), 0.0) for x, y in zip(kks, bts)]
        tinvs = _unit_lower_inverses(a_ab, eye)
        for (g, rows, _), t in zip(probs, tinvs):
            tinv_ref[g, rows, :] = t.astype(BF16)
        for (g, rows, r0), x, y in zip(probs, rs, bts):
            arb_ref[g, rows, :] = jnp.where(ri >= ci, _dot(x, y, _NT), 0.0).astype(BF16)
        for (g, rows, r0), x, y in zip(probs, kks, kts):
            a_ak = jnp.where(ri > ci, _dot(x, y, _NT), 0.0).astype(BF16)
            avx_ref[g, rows, :] = _dot(a_ak, vs_ref[g, rows, :])
        for (g, rows, r0), x, y in zip(probs, rs, kts):
            a_rk = jnp.where(ri >= ci, _dot(x, y, _NT), 0.0).astype(BF16)
            akv_ref[g, rows, :] = _dot(a_rk, vs_ref[g, rows, :])
        for g, rows, r0 in probs:
            khv_ref[g, rows, :] = _dot(khs_ref[g, rows, :], vs_ref[g, rows, :], _TN)
            gl = gc_ref[pl.ds(r0, 1), g * LANES:(g + 1) * LANES]
            gcol = jnp.sum(jnp.where(ri == ci, jnp.broadcast_to(gl, (n2, n2)), 0.0),
                           axis=1, keepdims=True)
            gful_ref[g, rows, :] = jnp.broadcast_to(gcol, (n2, n2))
        return carry

    lax.fori_loop(0, nch // group, prepare, 0)

    def advance(ch, carry):
        rows = pl.ds(pl.multiple_of(ch * n2, n2), n2)
        r0 = pl.multiple_of(ch * RWKV_CHUNK, RWKV_CHUNK)
        hs = [h_ref[g] for g in pairs]
        hbs = [h.astype(BF16) for h in hs]
        xs = [_dot(kks_ref[g, rows, :], hbs[g]) + avx_ref[g, rows, :] for g in pairs]
        us = [(-_dot(tinv_ref[g, rows, :], xs[g].astype(BF16))).astype(BF16) for g in pairs]
        for g in pairs:
            h_ref[g] = (hs[g] * gful_ref[g, rows, :] + _dot(bhs_ref[g, rows, :], us[g], _TN)
                        + khv_ref[g, rows, :])
        for g in pairs:
            o = (_dot(rs_ref[g, rows, :], hbs[g]) + _dot(arb_ref[g, rows, :], us[g])
                 + akv_ref[g, rows, :])
            o_ref[pl.ds(r0, RWKV_CHUNK), g * LANES:(g + 1) * LANES] = (
                o[0:RWKV_CHUNK] + o[RWKV_CHUNK:])
        return carry

    lax.fori_loop(0, nch, advance, 0)

    o = o_ref[...]
    mean = _head_sum(o, ones_bd) * (1.0 / HEAD_DIM)
    dev = o - mean
    var = _head_sum(dev * dev, ones_bd) * (1.0 / HEAD_DIM)
    o = dev * lax.rsqrt(var + GN_EPS) * lnw_ref[...] + lnb_ref[...]
    oc_ref[...] = (o + bonus) * gate


def _rwkv(pc, vfirst, prm, vmix_prm, bsz, tt):
    rows = pc.shape[0]
    nt = rows // bsz // tt
    row = lambda b, i: (b * nt + i, 0)
    const = lambda b, i: (0, 0)
    full = lambda a: pl.BlockSpec(a.shape, const)
    has_vmix = vfirst is not None
    inputs = [pc] + ([vfirst] if has_vmix else []) + list(prm) + (list(vmix_prm) if has_vmix else [])
    in_specs = ([pl.BlockSpec((tt, 4 * RWKV_WIDTH), row)]
                + ([pl.BlockSpec((tt, RWKV_WIDTH), row)] if has_vmix else [])
                + [full(a) for a in prm] + ([full(a) for a in vmix_prm] if has_vmix else []))
    n_out = 1 if has_vmix else 2
    tile = pltpu.VMEM((tt, RWKV_WIDTH), F32)
    stacked = lambda dt: pltpu.VMEM((RWKV_PAIRS, 2 * tt, LANES), dt)
    outs = pl.pallas_call(
        functools.partial(_rwkv_body, has_vmix=has_vmix),
        grid=(bsz, nt),
        in_specs=in_specs,
        out_specs=[pl.BlockSpec((tt, RWKV_WIDTH), row)] * n_out,
        out_shape=[jax.ShapeDtypeStruct((rows, RWKV_WIDTH), F32)] * n_out,
        scratch_shapes=[pltpu.VMEM((SUBLANES, 4 * RWKV_WIDTH), F32),
                        pltpu.VMEM((RWKV_PAIRS, LANES, LANES), F32), tile, tile]
                       + [stacked(BF16)] * 9 + [stacked(F32)] * 4,
        compiler_params=_params(("arbitrary", "arbitrary")),
        name="rwkv_vmix" if has_vmix else "rwkv",
    )(*inputs)
    return (outs[0], None) if has_vmix else (outs[0], outs[1])


def _outproj_body(oa_ref, ob_ref, oc_ref, w_ref, g_ref, h_ref, out_ref):
    m = jnp.concatenate([oa_ref[...].astype(BF16)] + [ob_ref[g] for g in range(FOX_PAIRS)]
                        + [oc_ref[...].astype(BF16)], axis=1)
    mixed = jnp.dot(m, w_ref[...], preferred_element_type=F32)
    out_ref[...] = h_ref[...] + _rms(mixed, g_ref[...])


def _outproj(oa, ob, oc, w, g, h, tm):
    rows = h.shape[0]
    row = lambda i: (i, 0)
    const = lambda i: (0, 0)
    return pl.pallas_call(
        _outproj_body,
        grid=(rows // tm,),
        in_specs=[pl.BlockSpec((tm, LRU_WIDTH), row),
                  pl.BlockSpec((FOX_PAIRS, tm, LANES), lambda i: (0, i, 0)),
                  pl.BlockSpec((tm, RWKV_WIDTH), row),
                  pl.BlockSpec((D_MODEL, D_MODEL), const),
                  pl.BlockSpec((1, D_MODEL), const),
                  pl.BlockSpec((tm, D_MODEL), row)],
        out_specs=pl.BlockSpec((tm, D_MODEL), row),
        out_shape=jax.ShapeDtypeStruct((rows, D_MODEL), F32),
        input_output_aliases={5: 0},
        compiler_params=_params(("arbitrary",)),
        name="outproj",
    )(oa, ob, oc, w, g, h)


def _ffn_body(h_ref, gpre_ref, wg_ref, wv_ref, cg_ref, cv_ref, wd_ref, gpost_ref, out_ref,
              xn_ref, u_ref, acc_ref, *, tiles_per_seq):
    i = pl.program_id(0)
    j = pl.program_id(1)
    tm = h_ref.shape[0]
    fc = wg_ref.shape[1]

    @pl.when(j == 0)
    def _():
        @pl.when(i % tiles_per_seq == 0)
        def _():
            xn_ref[0:SUBLANES, :] = jnp.zeros((SUBLANES, D_MODEL), BF16)

        @pl.when(i % tiles_per_seq != 0)
        def _():
            xn_ref[0:SUBLANES, :] = xn_ref[tm:tm + SUBLANES, :]

        xn_ref[SUBLANES:, :] = _rms(h_ref[...], gpre_ref[...]).astype(BF16)
        acc_ref[...] = jnp.zeros_like(acc_ref)

    xn = xn_ref[...]
    u_ref[:, 0:fc] = jnp.dot(xn, wg_ref[...], preferred_element_type=F32)
    u_ref[:, fc:2 * fc] = jnp.dot(xn, wv_ref[...], preferred_element_type=F32)
    fg = jnp.zeros((tm, fc), F32)
    fv = jnp.zeros((tm, fc), F32)
    for k in range(CONV_F):
        off = SUBLANES - (CONV_F - 1) + k
        fg = fg + cg_ref[k:k + 1, :] * u_ref[pl.ds(off, tm), 0:fc]
        fv = fv + cv_ref[k:k + 1, :] * u_ref[pl.ds(off, tm), fc:2 * fc]
    act = (_gelu_tanh(fg) * fv).astype(BF16)
    acc_ref[...] += jnp.dot(act, wd_ref[...], preferred_element_type=F32)

    @pl.when(j == pl.num_programs(1) - 1)
    def _():
        out_ref[...] = h_ref[...] + _rms(acc_ref[...], gpost_ref[...])


def _ffn(h, gpre, wup, conv, wdn, gpost, bsz, tm):
    rows = h.shape[0]
    nf = D_FF // FF_CHUNK
    row = lambda i, j: (i, 0)
    const = lambda i, j: (0, 0)
    return pl.pallas_call(
        functools.partial(_ffn_body, tiles_per_seq=rows // bsz // tm),
        grid=(rows // tm, nf),
        in_specs=[pl.BlockSpec((tm, D_MODEL), row),
                  pl.BlockSpec((1, D_MODEL), const),
                  pl.BlockSpec((D_MODEL, FF_CHUNK), lambda i, j: (0, j)),
                  pl.BlockSpec((D_MODEL, FF_CHUNK), lambda i, j: (0, nf + j)),
                  pl.BlockSpec((SUBLANES, FF_CHUNK), lambda i, j: (0, j)),
                  pl.BlockSpec((SUBLANES, FF_CHUNK), lambda i, j: (0, nf + j)),
                  pl.BlockSpec((FF_CHUNK, D_MODEL), lambda i, j: (j, 0)),
                  pl.BlockSpec((1, D_MODEL), const)],
        out_specs=pl.BlockSpec((tm, D_MODEL), row),
        out_shape=jax.ShapeDtypeStruct((rows, D_MODEL), F32),
        scratch_shapes=[pltpu.VMEM((tm + SUBLANES, D_MODEL), BF16),
                        pltpu.VMEM((tm + SUBLANES, 2 * FF_CHUNK), F32),
                        pltpu.VMEM((tm, D_MODEL), F32)],
        input_output_aliases={0: 0},
        compiler_params=_params(("arbitrary", "arbitrary")),
        name="ffn",
    )(h, gpre, wup, wup, conv, conv, wdn, gpost)


def _row(x):
    return x.reshape(1, -1).astype(F32)


def _pad_rows(x, n):
    return jnp.pad(x, ((0, n - x.shape[0]), (0, 0)))


def _block_diag(w):
    heads, d, _ = w.shape
    eye = jnp.eye(heads, dtype=w.dtype)
    return (w[:, :, None, :] * eye[:, None, :, None]).reshape(heads * d, heads * d)


def _pack_w_in(w):
    wa = w[:, :2 * LRU_WIDTH]
    o = 2 * LRU_WIDTH
    wq = w[:, o:o + FOX_WIDTH] * (HEAD_DIM ** -0.5)
    wk = w[:, o + FOX_WIDTH:o + 2 * FOX_WIDTH]
    wv = w[:, o + 2 * FOX_WIDTH:o + 3 * FOX_WIDTH]
    wf = w[:, o + 3 * FOX_WIDTH:o + 3 * FOX_WIDTH + FOX_HEADS]
    wc = w[:, o + 3 * FOX_WIDTH + FOX_HEADS:]
    wf = jnp.pad(wf, ((0, 0), (0, LANES - FOX_HEADS)))
    return jnp.concatenate([wa, wq, wk, wv, wf, wc], axis=1).astype(BF16)


def _bias_placement():
    place = np.zeros((BIAS_PIECES * LANES, 2 * FOX_HEADS * LANES), np.float32)
    ones = np.zeros((1, 2 * FOX_HEADS * LANES), np.float32)
    for hd in range(FOX_HEADS):
        q0 = hd * LANES + HEAD_DIM
        k0 = (FOX_HEADS + hd) * LANES + HEAD_DIM
        for j in range(BIAS_PIECES):
            place[j * LANES + hd, q0 + j] = 1.0
            ones[0, q0 + BIAS_PIECES + j] = 1.0
            ones[0, k0 + j] = 1.0
            place[j * LANES + hd, k0 + BIAS_PIECES + j] = -1.0
    return jnp.asarray(place, BF16), jnp.asarray(ones, F32)


def kernel(x, meta_tokens, norm_mix_pre, norm_mix_post, norm_ffn_pre, norm_ffn_post, w_in, w_out, lru_conv_w, lru_conv_b, lru_gate_a_w, lru_gate_a_b, lru_gate_x_w, lru_gate_x_b, lru_lambda, fox_f_bias, rwkv_mu, rwkv_w0, rwkv_w_up, rwkv_a0, rwkv_a_up, rwkv_g_up, rwkv_v0, rwkv_v_down, rwkv_v_up, rwkv_k_k, rwkv_k_a, rwkv_r_k, rwkv_ln_w, rwkv_ln_b, ffn_up, ffn_conv, ffn_down):
    bsz, seq, _ = x.shape
    depth = w_in.shape[0]
    t = N_META + seq
    t_pad = -(-t // LANES) * LANES
    tile = _time_tile(t_pad)
    rows = bsz * t_pad

    meta = jnp.broadcast_to(meta_tokens.astype(x.dtype)[None], (bsz, N_META, D_MODEL))
    h = jnp.concatenate([meta, x, jnp.zeros((bsz, t_pad - t, D_MODEL), x.dtype)], axis=1)
    h = h.reshape(rows, D_MODEL)

    place, ones = _bias_placement()
    v_first = None
    for l in range(depth):
        f_bias = jnp.pad(_row(fox_f_bias[l]), ((0, 0), (0, LANES - FOX_HEADS)))
        pa, qe, ke, ve, pc = _inproj(h, _row(norm_mix_pre[l]), _pack_w_in(w_in[l]), f_bias,
                                     place, ones, bsz, tile)

        lru_prm = (_pad_rows(lru_conv_w[l], SUBLANES), _row(lru_conv_b[l]),
                   _block_diag(lru_gate_a_w[l]).astype(BF16), _row(lru_gate_a_b[l]),
                   _block_diag(lru_gate_x_w[l]).astype(BF16), _row(lru_gate_x_b[l]),
                   _row(lru_lambda[l]))
        oa = _lru(pa, lru_prm, bsz, tile)

        ob = _fox(qe, ke, ve, bsz, tile)

        rwkv_prm = (_row(rwkv_mu[l]), _row(rwkv_w0[l]),
                    _pad_rows(rwkv_w_up[l], LANES), _row(rwkv_a0[l]),
                    jnp.pad(rwkv_a_up[l], ((W_RANK, 0), (0, 0))), rwkv_g_up[l],
                    _row(rwkv_k_k[l]), _row(rwkv_k_a[l]), _row(rwkv_r_k[l]),
                    _row(rwkv_ln_w[l]), _row(rwkv_ln_b[l]))
        vmix_prm = None
        if l > 0:
            vmix_prm = (_row(rwkv_v0[l - 1]),
                        jnp.pad(rwkv_v_down[l - 1], ((0, 0), (0, LANES - V_RANK))),
                        _pad_rows(rwkv_v_up[l - 1], LANES))
        oc, v_own = _rwkv(pc, v_first if l > 0 else None, rwkv_prm, vmix_prm, bsz, tile)
        if l == 0:
            v_first = v_own

        h = _outproj(oa, ob, oc, w_out[l].astype(BF16), _row(norm_mix_post[l]), h, tile)
        h = _ffn(h, _row(norm_ffn_pre[l]), ffn_up[l].astype(BF16),
                 _pad_rows(ffn_conv[l], SUBLANES), ffn_down[l].astype(BF16),
                 _row(norm_ffn_post[l]), bsz, tile)

    return h.reshape(bsz, t_pad, D_MODEL)[:, N_META:N_META + seq]
```

```python
import functools

import jax
import jax.numpy as jnp
from jax import lax
from jax.experimental import pallas as pl
from jax.experimental.pallas import tpu as pltpu

F32 = jnp.float32
BF16 = jnp.bfloat16

D_MODEL = 1024
N_META = 16
HEAD_DIM = 64
LANES = 128
SUBLANES = 8
LRU_WIDTH = 256
LRU_C = 8.0
LRU_BLOCK = 64
CONV_A = 4
FOX_WIDTH = 512
FOX_HEADS = 8
FOX_PAIRS = FOX_HEADS // 2
FOX_STEP_HEADS = 8
FOX_STRIP = 128
RWKV_WIDTH = 256
RWKV_PAIRS = RWKV_WIDTH // LANES
W_RANK = 64
A_RANK = 64
G_RANK = 128
V_RANK = 32
GN_EPS = 64e-5
LOG2_E = 1.4426950408889634
D_FF = 2816
FF_CHUNK = 256
CONV_F = 3
EPS = 1e-6
TIME_TILE = 640
RWKV_CHUNK = 64
INV_BLOCK = 16
PREP_GROUP = 5
VMEM_LIMIT = 56 * 1024 * 1024

COL_A = 0
COL_Q = COL_A + 2 * LRU_WIDTH
COL_K = COL_Q + FOX_WIDTH
COL_V = COL_K + FOX_WIDTH
COL_F = COL_V + FOX_WIDTH
COL_C = COL_F + LANES
N_PROJ = COL_C + 4 * RWKV_WIDTH
BIAS_PIECES = 3
LANE_ONE = HEAD_DIM


def _time_plan(t):
    if t <= TIME_TILE:
        t_pad = -(-t // LANES) * LANES
        return t_pad, t_pad
    return -(-t // TIME_TILE) * TIME_TILE, TIME_TILE


def _params(sem, flags=None):
    return pltpu.CompilerParams(dimension_semantics=sem, vmem_limit_bytes=VMEM_LIMIT, flags=flags)


def _rms(x, g):
    return x * lax.rsqrt(jnp.mean(x * x, axis=-1, keepdims=True) + EPS) * g


def _softplus(x):
    return jnp.maximum(x, 0.0) + jnp.log1p(jnp.exp(-jnp.abs(x)))


def _sigmoid(x):
    return 1.0 / (1.0 + jnp.exp(-x))


def _gelu_tanh(x):
    return 0.5 * x * (1.0 + jnp.tanh(0.7978845608028654 * (x + 0.044715 * x * x * x)))


def _split(x):
    hi = x.astype(BF16)
    lo = (x - hi.astype(F32)).astype(BF16)
    return hi, lo


_NN = (((1,), (0,)), ((), ()))
_NT = (((1,), (1,)), ((), ()))
_TN = (((0,), (0,)), ((), ()))


def _mm3(a, b, dims=_NN):
    ah, al = a
    bh, bl = b
    d = functools.partial(lax.dot_general, dimension_numbers=dims, preferred_element_type=F32)
    return d(ah, bh) + (d(ah, bl) + d(al, bh))


def _mmf(a, b, dims=_NN):
    return _mm3(_split(a), _split(b), dims)


def _dot(a, b, dims=_NN):
    return lax.dot_general(a, b, dimension_numbers=dims, preferred_element_type=F32)


def _scan_steps(n):
    d = 1
    while d < n:
        yield d
        d *= 2


def _inproj_body(h_ref, g_ref, w_ref, fb_ref, cw_ref, cb_ref, wa_ref, ba_ref, wx_ref, bx_ref,
                 lam_ref, oa_ref, q_ref, k_ref, v_ref, pc_ref, ccar_ref, ext_ref, hcar_ref, lin_ref,
                 *, tiles_per_seq):
    tm = h_ref.shape[0]
    seq_start = pl.program_id(0) % tiles_per_seq == 0
    xn = _rms(h_ref[...], g_ref[...]).astype(BF16)

    def proj(lo, n):
        return jnp.dot(xn, w_ref[:, lo:lo + n], preferred_element_type=F32)

    def projections():
        step = 2 * LANES
        for c0 in range(0, 4 * RWKV_WIDTH, step):
            pc_ref[:, c0:c0 + step] = proj(COL_C + c0, step)
            yield

        @pl.when(seq_start)
        def _():
            ccar_ref[...] = jnp.zeros_like(ccar_ref)

        c = -_softplus(-(proj(COL_F, LANES) + fb_ref[...]))
        row = lax.broadcasted_iota(jnp.int32, (tm, LANES), 0)
        for d in _scan_steps(tm):
            c = c + jnp.where(row >= d, pltpu.roll(c, d, axis=0), 0.0)
        c = c + ccar_ref[0:1, :]
        ccar_ref[...] = jnp.broadcast_to(c[tm - 1:tm, :], ccar_ref.shape)

        pieces = []
        rest = c * LOG2_E
        for _ in range(BIAS_PIECES):
            piece = rest.astype(BF16).astype(F32)
            pieces.append(piece)
            rest = rest - piece
        yield

        lane = lax.broadcasted_iota(jnp.int32, (tm, LANES), 1)
        spare = lane - HEAD_DIM
        in_first = (spare >= 0) & (spare < BIAS_PIECES)
        in_second = (spare >= BIAS_PIECES) & (spare < 2 * BIAS_PIECES)
        q_ones = jnp.where(in_second, 1.0, 0.0)
        k_ones = jnp.where(in_first, 1.0, 0.0)
        v_extra = jnp.where(lane == LANE_ONE, 1.0, 0.0)
        qkv = []
        for col in (COL_Q, COL_K, COL_V):
            qkv.append(proj(col, FOX_WIDTH))
            yield
        q, k, v = qkv
        for hd in range(FOX_HEADS):
            g = hd // 2

            def head_tile(x):
                x = x[:, g * LANES:(g + 1) * LANES]
                return pltpu.roll(x, HEAD_DIM, axis=1) if hd % 2 else x

            cols = [jnp.broadcast_to(p[:, hd:hd + 1], (tm, LANES)) for p in pieces]
            sel = cols[BIAS_PIECES - 1]
            for j in range(BIAS_PIECES - 1):
                sel = jnp.where(spare % BIAS_PIECES == j, cols[j], sel)
            qx = jnp.where(in_first, sel, q_ones)
            kx = jnp.where(in_second, -sel, k_ones)
            q_ref[hd] = jnp.where(lane < HEAD_DIM, head_tile(q), qx).astype(BF16)
            k_ref[hd] = jnp.where(lane < HEAD_DIM, head_tile(k), kx).astype(BF16)
            v_ref[hd] = jnp.where(lane < HEAD_DIM, head_tile(v), v_extra).astype(BF16)
            yield

    pa = proj(COL_A, 2 * LRU_WIDTH)
    streams = [_lru_tile(pa, seq_start, cw_ref, cb_ref, wa_ref, ba_ref, wx_ref, bx_ref, lam_ref,
                         ext_ref, hcar_ref, lin_ref, oa_ref),
               projections()]
    while streams:
        streams = [s for s in streams if next(s, True) is None]


def _inproj(h, g, w, fb, lru_prm, bsz, tm):
    rows = h.shape[0]
    row = lambda i: (i, 0)
    const = lambda i: (0, 0)
    head = pl.BlockSpec((FOX_HEADS, tm, LANES), lambda i: (0, i, 0))
    head_shape = jax.ShapeDtypeStruct((FOX_HEADS, rows, LANES), BF16)
    return pl.pallas_call(
        functools.partial(_inproj_body, tiles_per_seq=rows // bsz // tm),
        grid=(rows // tm,),
        in_specs=[pl.BlockSpec((tm, D_MODEL), row),
                  pl.BlockSpec((1, D_MODEL), const),
                  pl.BlockSpec((D_MODEL, N_PROJ), const),
                  pl.BlockSpec(fb.shape, const)]
                 + [pl.BlockSpec(a.shape, const) for a in lru_prm],
        out_specs=[pl.BlockSpec((tm, LRU_WIDTH), row), head, head, head,
                   pl.BlockSpec((tm, 4 * RWKV_WIDTH), row)],
        out_shape=[jax.ShapeDtypeStruct((rows, LRU_WIDTH), F32),
                   head_shape, head_shape, head_shape,
                   jax.ShapeDtypeStruct((rows, 4 * RWKV_WIDTH), F32)],
        scratch_shapes=[pltpu.VMEM((SUBLANES, LANES), F32),
                        pltpu.VMEM((tm + SUBLANES, LRU_WIDTH), F32),
                        pltpu.VMEM((SUBLANES, LRU_WIDTH), F32),
                        pltpu.VMEM((tm, 4 * LRU_WIDTH), F32)],
        compiler_params=_params(("arbitrary",)),
        name="inproj",
    )(h, g, w, fb, *lru_prm)


def _lru_tile(pa, seq_start, cw_ref, cb_ref, wa_ref, ba_ref, wx_ref, bx_ref, lam_ref,
              ext_ref, hcar_ref, lin_ref, oa_ref):
    tt = pa.shape[0]

    @pl.when(seq_start)
    def _():
        ext_ref[0:SUBLANES, :] = jnp.zeros((SUBLANES, LRU_WIDTH), F32)
        hcar_ref[...] = jnp.zeros_like(hcar_ref)

    xa = pa[:, 0:LRU_WIDTH]
    ya = pa[:, LRU_WIDTH:2 * LRU_WIDTH]
    ext_ref[SUBLANES:, :] = xa
    u = cb_ref[...]
    for k in range(CONV_A):
        off = SUBLANES - (CONV_A - 1) + k
        u = u + cw_ref[k:k + 1, :] * ext_ref[pl.ds(off, tt), :]
    ext_ref[0:SUBLANES, :] = xa[tt - SUBLANES:, :]
    ub = u.astype(BF16)
    lin_ref[:, 0:LRU_WIDTH] = jnp.dot(ub, wa_ref[...], preferred_element_type=F32) + ba_ref[...]
    lin_ref[:, LRU_WIDTH:2 * LRU_WIDTH] = (jnp.dot(ub, wx_ref[...], preferred_element_type=F32)
                                           + bx_ref[...])
    lin_ref[:, 2 * LRU_WIDTH:3 * LRU_WIDTH] = u
    lin_ref[:, 3 * LRU_WIDTH:4 * LRU_WIDTH] = ya
    yield

    neg_rate = (-LRU_C) * _softplus(-lam_ref[...])
    row = lax.broadcasted_iota(jnp.int32, (LRU_BLOCK, LRU_WIDTH), 0)
    carry = hcar_ref[0:1, :]
    for r0 in range(0, tt, LRU_BLOCK):
        rows = slice(r0, r0 + LRU_BLOCK)
        r = _sigmoid(lin_ref[rows, 0:LRU_WIDTH])
        ig = _sigmoid(lin_ref[rows, LRU_WIDTH:2 * LRU_WIDTH])
        log_a = neg_rate * r
        a = jnp.exp(log_a)
        b = (jnp.sqrt(-jnp.tanh(log_a) * (a * a + 1.0))
             * (ig * lin_ref[rows, 2 * LRU_WIDTH:3 * LRU_WIDTH]))
        for d in _scan_steps(LRU_BLOCK):
            live = row >= d
            b = b + a * jnp.where(live, pltpu.roll(b, d, axis=0), 0.0)
            a = a * jnp.where(live, pltpu.roll(a, d, axis=0), 1.0)
        h = b + a * carry
        carry = h[LRU_BLOCK - 1:LRU_BLOCK, :]
        oa_ref[rows, :] = h * _gelu_tanh(lin_ref[rows, 3 * LRU_WIDTH:4 * LRU_WIDTH])
        yield
    hcar_ref[...] = jnp.broadcast_to(carry, hcar_ref.shape)


def _fox_body(qi_ref, ki_ref, q_ref, k_ref, v_ref, o_ref,
              m_ref, acc_ref, mask_ref, s_ref, p_ref, mb_ref):
    p = pl.program_id(2)
    qi = qi_ref[p]
    ki = ki_ref[p]
    tq = q_ref.shape[1]
    tk = k_ref.shape[1]

    @pl.when((pl.program_id(0) == 0) & (pl.program_id(1) == 0) & (p == 0))
    def _():
        causal = (lax.broadcasted_iota(jnp.int32, (tq, tk), 1)
                  <= lax.broadcasted_iota(jnp.int32, (tq, tk), 0))
        mask_ref[...] = jnp.where(causal, 0.0, -jnp.inf)

    @pl.when(ki == qi)
    def _():
        m_ref[...] = jnp.full_like(m_ref, -jnp.inf)
        acc_ref[...] = jnp.zeros_like(acc_ref)

    def update(masked):
        def scores(e):
            s = lax.dot_general(q_ref[e], k_ref[e], _NT, preferred_element_type=F32)
            if masked:
                s = s + mask_ref[...]
            s_ref[e] = s
            mx = s[:, 0:LANES]
            for c0 in range(LANES, tk, LANES):
                mx = jnp.maximum(mx, s[:, c0:c0 + LANES])
            mb_ref[e] = mx

        for e in range(FOX_STEP_HEADS):
            scores(e)
        for e in range(FOX_STEP_HEADS):
            m_prev = m_ref[e]
            m_new = jnp.maximum(m_prev, jnp.max(mb_ref[e], axis=1, keepdims=True))
            m_ref[e] = m_new
            mb_ref[e] = jnp.broadcast_to(m_new, (tq, LANES))
            acc_ref[e] = jnp.exp2(m_prev - m_new) * acc_ref[e]
            for r0 in range(0, tq, FOX_STRIP):
                rows = slice(r0, r0 + FOX_STRIP)
                mb = mb_ref[e, rows, :]
                for c0 in range(0, tk, LANES):
                    cols = slice(c0, c0 + LANES)
                    p_ref[e, rows, cols] = jnp.exp2(s_ref[e, rows, cols] - mb).astype(BF16)
            acc_ref[e] += jnp.dot(p_ref[e], v_ref[e], preferred_element_type=F32)

    @pl.when(ki < qi)
    def _():
        update(False)

    @pl.when(ki == qi)
    def _():
        update(True)

    @pl.when(ki == 0)
    def _():
        lane = lax.broadcasted_iota(jnp.int32, (tq, LANES), 1)
        for g in range(FOX_STEP_HEADS // 2):
            o0, o1 = (acc_ref[e] / acc_ref[e][:, LANE_ONE:LANE_ONE + 1] for e in (2 * g, 2 * g + 1))
            o_ref[g] = jnp.where(lane < HEAD_DIM, o0,
                                 pltpu.roll(o1, HEAD_DIM, axis=1)).astype(BF16)


def _fox(qe, ke, ve, bsz, tq):
    rows = qe.shape[1]
    nq = rows // bsz // tq
    nh = FOX_STEP_HEADS
    pairs = [(i, j) for i in range(nq) for j in reversed(range(i + 1))]
    qi_tab = jnp.asarray([p[0] for p in pairs], jnp.int32)
    ki_tab = jnp.asarray([p[1] for p in pairs], jnp.int32)
    grid_spec = pltpu.PrefetchScalarGridSpec(
        num_scalar_prefetch=2,
        grid=(bsz, FOX_HEADS // nh, len(pairs)),
        in_specs=[
            pl.BlockSpec((nh, tq, LANES), lambda b, g, p, qi, ki: (g, b * nq + qi[p], 0)),
            pl.BlockSpec((nh, tq, LANES), lambda b, g, p, qi, ki: (g, b * nq + ki[p], 0)),
            pl.BlockSpec((nh, tq, LANES), lambda b, g, p, qi, ki: (g, b * nq + ki[p], 0)),
        ],
        out_specs=pl.BlockSpec((nh // 2, tq, LANES),
                               lambda b, g, p, qi, ki: (g, b * nq + qi[p], 0)),
        scratch_shapes=[pltpu.VMEM((nh, tq, 1), F32), pltpu.VMEM((nh, tq, LANES), F32),
                        pltpu.VMEM((tq, tq), F32), pltpu.VMEM((nh, tq, tq), F32),
                        pltpu.VMEM((nh, tq, tq), BF16), pltpu.VMEM((nh, tq, LANES), F32)],
    )
    return pl.pallas_call(
        _fox_body,
        grid_spec=grid_spec,
        out_shape=jax.ShapeDtypeStruct((FOX_PAIRS, rows, LANES), BF16),
        compiler_params=_params(("arbitrary", "arbitrary", "arbitrary"),
                                flags=None),
        name="fox",
    )(qi_tab, ki_tab, qe, ke, ve)


def _head_sum(x, ones_bd):
    hi, lo = _split(x)
    d = functools.partial(jnp.dot, preferred_element_type=F32)
    return d(hi, ones_bd) + d(lo, ones_bd)


def _block_diag2(y):
    zero = jnp.zeros((y.shape[0], LANES), y.dtype)
    return jnp.concatenate([jnp.concatenate([y[:, :LANES], zero], axis=1),
                            jnp.concatenate([zero, y[:, LANES:]], axis=1)], axis=0)


def _twin(x, y, dims=_NN):
    return _dot(x, _block_diag2(y), dims)


def _twin_tn(x, y):
    stacked = jnp.concatenate([x[:, :LANES], x[:, LANES:]], axis=0)
    return _dot(stacked, _block_diag2(y), _TN)


def _unit_lower_inverses(mats, eye, out):
    r = lax.broadcasted_iota(jnp.int32, eye.shape, 0) // INV_BLOCK
    c = (lax.broadcasted_iota(jnp.int32, eye.shape, 1) % LANES) // INV_BLOCK
    dgs = [jnp.where(r == c, a, 0.0) for a in mats]
    offs = [(a - dg).astype(BF16) for a, dg in zip(mats, dgs)]
    pws = [dg.astype(BF16) for dg in dgs]
    accs = [eye - dg for dg in dgs]
    for _ in range(3):
        pws = [_twin(pw, pw).astype(BF16) for pw in pws]
        yield
        accs = [acc + _twin(acc.astype(BF16), pw) for acc, pw in zip(accs, pws)]
        yield
    dinvs = [acc.astype(BF16) for acc in accs]
    nns = [_twin(dinv, off) for dinv, off in zip(dinvs, offs)]
    yield
    sqs = [_twin(nn.astype(BF16), nn.astype(BF16)).astype(BF16) for nn in nns]
    yield
    m2s = [(eye - nn) + _twin((eye - nn).astype(BF16), sq) for nn, sq in zip(nns, sqs)]
    yield
    out.extend(_twin(m2.astype(BF16), dinv) for m2, dinv in zip(m2s, dinvs))


def _rwkv_body(*refs, has_vmix):
    it = iter(refs)
    pc_ref = next(it)
    vf_ref = next(it) if has_vmix else None
    (mu_ref, w0_ref, wup_ref, a0_ref, aup_ref, gup_ref, kk_ref, ka_ref, rk_ref,
     lnw_ref, lnb_ref) = (next(it) for _ in range(11))
    if has_vmix:
        v0_ref, vdn_ref, vup_ref = (next(it) for _ in range(3))
    oc_ref = next(it)
    vown_ref = None if has_vmix else next(it)
    prev_ref, h_ref, gc_ref, o_ref = (next(it) for _ in range(4))
    kks_ref, rs_ref, bts_ref, kts_ref, bhs_ref, khs_ref, vs_ref = (next(it) for _ in range(7))
    tinv_ref, arb_ref, avx_ref, akv_ref, khv_ref, gful_ref = (next(it) for _ in range(6))

    tt = pc_ref.shape[0]
    nch = tt // RWKV_CHUNK
    width = RWKV_WIDTH

    @pl.when(pl.program_id(1) == 0)
    def _():
        prev_ref[...] = jnp.zeros_like(prev_ref)
        h_ref[...] = jnp.zeros_like(h_ref)

    pc = pc_ref[...]
    row = lax.broadcasted_iota(jnp.int32, pc.shape, 0)
    shifted = jnp.where(row == 0, prev_ref[SUBLANES - 1:SUBLANES, :], pltpu.roll(pc, 1, axis=0))
    prev_ref[...] = pc[tt - SUBLANES:, :]
    pc = pc + (shifted - pc) * mu_ref[...]
    r = pc[:, 0:width]
    k = pc[:, width:2 * width]
    v = pc[:, 2 * width:3 * width]
    wa_dn = pc[:, 3 * width:3 * width + LANES]
    g_dn = pc[:, 3 * width + LANES:3 * width + 2 * LANES]

    rr = lax.broadcasted_iota(jnp.int32, (width, width), 0) // HEAD_DIM
    cc = lax.broadcasted_iota(jnp.int32, (width, width), 1) // HEAD_DIM
    ones_bd = jnp.where(rr == cc, 1.0, 0.0).astype(BF16)

    if has_vmix:
        vm = _mmf(_mmf(v, vdn_ref[...]), vup_ref[...])
        v = v + (vf_ref[...] - v) * _sigmoid(v0_ref[...] + vm)
    else:
        vown_ref[...] = v
    w_log = -_softplus(-(w0_ref[...] + _mmf(jnp.tanh(wa_dn), wup_ref[...]))) - 0.5
    lw = -jnp.exp(w_log)
    a = _sigmoid(a0_ref[...] + _mmf(wa_dn, aup_ref[...]))
    gate = _mmf(_sigmoid(g_dn), gup_ref[...])
    kkr = k * kk_ref[...]
    kk = kkr / jnp.maximum(jnp.sqrt(_head_sum(kkr * kkr, ones_bd)), 1e-12)
    k = k * (1.0 + (a - 1.0) * ka_ref[...])
    b = a * kk
    bonus = _head_sum(r * k * rk_ref[...], ones_bd) * v

    cum = lw
    rowc = lax.broadcasted_iota(jnp.int32, (tt, width), 0) % RWKV_CHUNK
    for d in _scan_steps(RWKV_CHUNK):
        cum = cum + jnp.where(rowc >= d, pltpu.roll(cum, d, axis=0), 0.0)
    cum3 = cum.reshape(nch, RWKV_CHUNK, width)
    clast = jnp.broadcast_to(cum3[:, RWKV_CHUNK - 1:RWKV_CHUNK, :], cum3.shape).reshape(tt, width)
    ginv = jnp.exp(-cum)
    ghat = jnp.exp(clast - cum)
    gc_ref[...] = jnp.exp(clast)

    n2 = 2 * RWKV_CHUNK
    lane3 = lax.broadcasted_iota(jnp.int32, (nch, RWKV_CHUNK, width), 2) % LANES

    def stack(ref, x):
        xp = x.reshape(nch, RWKV_CHUNK, width)
        st = jnp.concatenate([jnp.where(lane3 < HEAD_DIM, xp, 0.0),
                              jnp.where(lane3 < HEAD_DIM, 0.0, xp)], axis=1)
        ref[...] = st.reshape(nch * n2, width).astype(BF16)

    stack(kks_ref, kk * jnp.exp(cum - lw))
    stack(rs_ref, r * jnp.exp(cum))
    stack(bts_ref, b * ginv)
    stack(kts_ref, k * ginv)
    stack(bhs_ref, b * ghat)
    stack(khs_ref, k * ghat)
    stack(vs_ref, v)

    ri = lax.broadcasted_iota(jnp.int32, (n2, width), 0)
    ci = lax.broadcasted_iota(jnp.int32, (n2, width), 1) % LANES
    eye = jnp.where(ri == ci, 1.0, 0.0)

    group = max(d for d in range(1, PREP_GROUP + 1) if nch % d == 0)

    def prepare(it):
        probs = [(slice((it * group + j) * n2, (it * group + j + 1) * n2),
                  (it * group + j) * RWKV_CHUNK) for j in range(group)]
        kks = [kks_ref[rows, :] for rows, _ in probs]
        rs = [rs_ref[rows, :] for rows, _ in probs]
        bts = [bts_ref[rows, :] for rows, _ in probs]
        kts = [kts_ref[rows, :] for rows, _ in probs]
        quads = [_dot(jnp.concatenate([x, z], axis=0),
                      jnp.concatenate([_block_diag2(y), _block_diag2(w)], axis=0), _NT)
                 for x, z, y, w in zip(kks, rs, bts, kts)]
        a_ab = [jnp.where(ri > ci, q[0:n2, 0:width], 0.0) for q in quads]
        yield
        tinvs = []
        yield from _unit_lower_inverses(a_ab, eye, tinvs)
        for (rows, _), t in zip(probs, tinvs):
            tinv_ref[rows, :] = t.astype(BF16)
        yield
        for (rows, _), q in zip(probs, quads):
            arb_ref[rows, :] = jnp.where(ri >= ci, q[n2:2 * n2, 0:width], 0.0).astype(BF16)
        yield
        for (rows, _), q in zip(probs, quads):
            a_k = jnp.concatenate([jnp.where(ri > ci, q[0:n2, width:2 * width], 0.0),
                                   jnp.where(ri >= ci, q[n2:2 * n2, width:2 * width], 0.0)],
                                  axis=0).astype(BF16)
            av = _twin(a_k, vs_ref[rows, :])
            avx_ref[rows, :] = av[0:n2]
            akv_ref[rows, :] = av[n2:2 * n2]
        yield
        for rows, r0 in probs:
            khv_ref[rows, :] = _twin_tn(khs_ref[rows, :], vs_ref[rows, :])
            diag = jnp.where(ri == ci, jnp.broadcast_to(gc_ref[pl.ds(r0, 1), :], (n2, width)), 0.0)
            gful_ref[rows, :] = jnp.concatenate(
                [jnp.broadcast_to(jnp.sum(diag[:, g * LANES:(g + 1) * LANES], axis=1,
                                          keepdims=True), (n2, LANES))
                 for g in range(RWKV_PAIRS)], axis=1)

    def advance(ch):
        rows = slice(ch * n2, (ch + 1) * n2)
        r0 = ch * RWKV_CHUNK
        tiles = [slice(g * LANES, (g + 1) * LANES) for g in range(RWKV_PAIRS)]
        hs = [h_ref[:, t] for t in tiles]
        hbs = [h.astype(BF16) for h in hs]
        khs = [_dot(jnp.concatenate([kks_ref[rows, t], rs_ref[rows, t]], axis=0), hb)
               for t, hb in zip(tiles, hbs)]
        xs = [kh[0:n2] + avx_ref[rows, t] for t, kh in zip(tiles, khs)]
        yield
        us = [(-_dot(tinv_ref[rows, t], x.astype(BF16))).astype(BF16) for t, x in zip(tiles, xs)]
        yield
        for t, h, u in zip(tiles, hs, us):
            h_ref[:, t] = h * gful_ref[rows, t] + _dot(bhs_ref[rows, t], u, _TN) + khv_ref[rows, t]
        for t, kh, u in zip(tiles, khs, us):
            o = kh[n2:2 * n2] + _dot(arb_ref[rows, t], u) + akv_ref[rows, t]
            o_ref[pl.ds(r0, RWKV_CHUNK), t] = o[0:RWKV_CHUNK] + o[RWKV_CHUNK:]
        yield

    def advance_group(it):
        for j in range(group):
            yield from advance(it * group + j)

    for _ in prepare(0):
        pass
    for it in range(nch // group):
        streams = [advance_group(it)]
        if it + 1 < nch // group:
            streams.append(prepare(it + 1))
        while streams:
            streams = [s for s in streams if next(s, True) is None]

    o = o_ref[...]
    mean = _head_sum(o, ones_bd) * (1.0 / HEAD_DIM)
    dev = o - mean
    var = _head_sum(dev * dev, ones_bd) * (1.0 / HEAD_DIM)
    o = dev * lax.rsqrt(var + GN_EPS) * lnw_ref[...] + lnb_ref[...]
    oc_ref[...] = (o + bonus) * gate


def _rwkv(pc, vfirst, prm, vmix_prm, bsz, tt):
    rows = pc.shape[0]
    nt = rows // bsz // tt
    row = lambda b, i: (b * nt + i, 0)
    const = lambda b, i: (0, 0)
    full = lambda a: pl.BlockSpec(a.shape, const)
    has_vmix = vfirst is not None
    inputs = [pc] + ([vfirst] if has_vmix else []) + list(prm) + (list(vmix_prm) if has_vmix else [])
    in_specs = ([pl.BlockSpec((tt, 4 * RWKV_WIDTH), row)]
                + ([pl.BlockSpec((tt, RWKV_WIDTH), row)] if has_vmix else [])
                + [full(a) for a in prm] + ([full(a) for a in vmix_prm] if has_vmix else []))
    n_out = 1 if has_vmix else 2
    tile = pltpu.VMEM((tt, RWKV_WIDTH), F32)
    stacked = lambda dt: pltpu.VMEM((2 * tt, RWKV_WIDTH), dt)
    outs = pl.pallas_call(
        functools.partial(_rwkv_body, has_vmix=has_vmix),
        grid=(bsz, nt),
        in_specs=in_specs,
        out_specs=[pl.BlockSpec((tt, RWKV_WIDTH), row)] * n_out,
        out_shape=[jax.ShapeDtypeStruct((rows, RWKV_WIDTH), F32)] * n_out,
        scratch_shapes=[pltpu.VMEM((SUBLANES, 4 * RWKV_WIDTH), F32),
                        pltpu.VMEM((2 * RWKV_CHUNK, RWKV_WIDTH), F32), tile, tile]
                       + [stacked(BF16)] * 9 + [stacked(F32)] * 4,
        compiler_params=_params(("arbitrary", "arbitrary")),
        name="rwkv_vmix" if has_vmix else "rwkv",
    )(*inputs)
    return (outs[0], None) if has_vmix else (outs[0], outs[1])


def _tail_body(oa_ref, ob_ref, oc_ref, wout_ref, gmix_ref, h_ref, gpre_ref, wup_ref, conv_ref,
               wdn_ref, gpost_ref, out_ref, h1_ref, xn_ref, u_ref, act_ref, *, tiles_per_seq):
    i = pl.program_id(0)
    tm = h_ref.shape[0]
    fc = FF_CHUNK

    m = jnp.concatenate([oa_ref[...].astype(BF16)] + [ob_ref[g] for g in range(FOX_PAIRS)]
                        + [oc_ref[...].astype(BF16)], axis=1)
    mixed = jnp.dot(m, wout_ref[...], preferred_element_type=F32)
    h1_ref[...] = h_ref[...] + _rms(mixed, gmix_ref[...])

    @pl.when(i % tiles_per_seq == 0)
    def _():
        xn_ref[0:SUBLANES, :] = jnp.zeros((SUBLANES, D_MODEL), BF16)

    @pl.when(i % tiles_per_seq != 0)
    def _():
        xn_ref[0:SUBLANES, :] = xn_ref[tm:tm + SUBLANES, :]

    xn_ref[SUBLANES:, :] = _rms(h1_ref[...], gpre_ref[...]).astype(BF16)
    xn = xn_ref[...]
    for j in range(D_FF // fc):
        gcols = slice(j * fc, (j + 1) * fc)
        vcols = slice(D_FF + j * fc, D_FF + (j + 1) * fc)
        u = u_ref.at[j % 2]
        u[:, 0:fc] = jnp.dot(xn, wup_ref[:, gcols], preferred_element_type=F32)
        u[:, fc:2 * fc] = jnp.dot(xn, wup_ref[:, vcols], preferred_element_type=F32)
        fg = jnp.zeros((tm, fc), F32)
        fv = jnp.zeros((tm, fc), F32)
        for k in range(CONV_F):
            off = SUBLANES - (CONV_F - 1) + k
            fg = fg + conv_ref[k:k + 1, gcols] * u[pl.ds(off, tm), 0:fc]
            fv = fv + conv_ref[k:k + 1, vcols] * u[pl.ds(off, tm), fc:2 * fc]
        act_ref[:, gcols] = (_gelu_tanh(fg) * fv).astype(BF16)
    y = jnp.dot(act_ref[...], wdn_ref[...], preferred_element_type=F32)
    out_ref[...] = h1_ref[...] + _rms(y, gpost_ref[...])


def _tail(oa, ob, oc, wout, gmix, h, gpre, wup, conv, wdn, gpost, bsz, tm):
    rows = h.shape[0]
    row = lambda i: (i, 0)
    const = lambda i: (0, 0)
    resident = lambda a: pl.BlockSpec(a.shape, const, pipeline_mode=pl.Buffered(1))
    gain = pl.BlockSpec((1, D_MODEL), const)
    return pl.pallas_call(
        functools.partial(_tail_body, tiles_per_seq=rows // bsz // tm),
        grid=(rows // tm,),
        in_specs=[pl.BlockSpec((tm, LRU_WIDTH), row),
                  pl.BlockSpec((FOX_PAIRS, tm, LANES), lambda i: (0, i, 0)),
                  pl.BlockSpec((tm, RWKV_WIDTH), row),
                  resident(wout), gain,
                  pl.BlockSpec((tm, D_MODEL), row), gain,
                  resident(wup), resident(conv), resident(wdn), gain],
        out_specs=pl.BlockSpec((tm, D_MODEL), row),
        out_shape=jax.ShapeDtypeStruct((rows, D_MODEL), F32),
        scratch_shapes=[pltpu.VMEM((tm, D_MODEL), F32),
                        pltpu.VMEM((tm + SUBLANES, D_MODEL), BF16),
                        pltpu.VMEM((2, tm + SUBLANES, 2 * FF_CHUNK), F32),
                        pltpu.VMEM((tm, D_FF), BF16)],
        input_output_aliases={5: 0},
        compiler_params=_params(("arbitrary",)),
        name="tail",
    )(oa, ob, oc, wout, gmix, h, gpre, wup, conv, wdn, gpost)


def _row(x):
    return x.reshape(1, -1).astype(F32)


def _pad_rows(x, n):
    return jnp.pad(x, ((0, n - x.shape[0]), (0, 0)))


def _block_diag(w):
    heads, d, _ = w.shape
    eye = jnp.eye(heads, dtype=w.dtype)
    return (w[:, :, None, :] * eye[:, None, :, None]).reshape(heads * d, heads * d)


def _pack_w_in(w):
    wa = w[:, :2 * LRU_WIDTH]
    o = 2 * LRU_WIDTH
    wq = w[:, o:o + FOX_WIDTH] * (HEAD_DIM ** -0.5 * LOG2_E)
    wk = w[:, o + FOX_WIDTH:o + 2 * FOX_WIDTH]
    wv = w[:, o + 2 * FOX_WIDTH:o + 3 * FOX_WIDTH]
    wf = w[:, o + 3 * FOX_WIDTH:o + 3 * FOX_WIDTH + FOX_HEADS]
    wc = w[:, o + 3 * FOX_WIDTH + FOX_HEADS:]
    wf = jnp.pad(wf, ((0, 0), (0, LANES - FOX_HEADS)))
    return jnp.concatenate([wa, wq, wk, wv, wf, wc], axis=1).astype(BF16)


def kernel(x, meta_tokens, norm_mix_pre, norm_mix_post, norm_ffn_pre, norm_ffn_post, w_in, w_out, lru_conv_w, lru_conv_b, lru_gate_a_w, lru_gate_a_b, lru_gate_x_w, lru_gate_x_b, lru_lambda, fox_f_bias, rwkv_mu, rwkv_w0, rwkv_w_up, rwkv_a0, rwkv_a_up, rwkv_g_up, rwkv_v0, rwkv_v_down, rwkv_v_up, rwkv_k_k, rwkv_k_a, rwkv_r_k, rwkv_ln_w, rwkv_ln_b, ffn_up, ffn_conv, ffn_down):
    bsz, seq, _ = x.shape
    depth = w_in.shape[0]
    t = N_META + seq
    t_pad, tile = _time_plan(t)
    rows = bsz * t_pad

    meta = jnp.broadcast_to(meta_tokens.astype(x.dtype)[None], (bsz, N_META, D_MODEL))
    h = jnp.concatenate([meta, x, jnp.zeros((bsz, t_pad - t, D_MODEL), x.dtype)], axis=1)
    h = h.reshape(rows, D_MODEL)

    v_first = None
    for l in range(depth):
        f_bias = jnp.pad(_row(fox_f_bias[l]), ((0, 0), (0, LANES - FOX_HEADS)))
        lru_prm = (_pad_rows(lru_conv_w[l], SUBLANES), _row(lru_conv_b[l]),
                   _block_diag(lru_gate_a_w[l]).astype(BF16), _row(lru_gate_a_b[l]),
                   _block_diag(lru_gate_x_w[l]).astype(BF16), _row(lru_gate_x_b[l]),
                   _row(lru_lambda[l]))
        oa, qe, ke, ve, pc = _inproj(h, _row(norm_mix_pre[l]), _pack_w_in(w_in[l]), f_bias,
                                     lru_prm, bsz, tile)

        ob = _fox(qe, ke, ve, bsz, tile)

        rwkv_prm = (_row(rwkv_mu[l]), _row(rwkv_w0[l]),
                    _pad_rows(rwkv_w_up[l], LANES), _row(rwkv_a0[l]),
                    jnp.pad(rwkv_a_up[l], ((W_RANK, 0), (0, 0))), rwkv_g_up[l],
                    _row(rwkv_k_k[l]), _row(rwkv_k_a[l]), _row(rwkv_r_k[l]),
                    _row(rwkv_ln_w[l]), _row(rwkv_ln_b[l]))
        vmix_prm = None
        if l > 0:
            vmix_prm = (_row(rwkv_v0[l - 1]),
                        jnp.pad(rwkv_v_down[l - 1], ((0, 0), (0, LANES - V_RANK))),
                        _pad_rows(rwkv_v_up[l - 1], LANES))
        oc, v_own = _rwkv(pc, v_first if l > 0 else None, rwkv_prm, vmix_prm, bsz, tile)
        if l == 0:
            v_first = v_own

        h = _tail(oa, ob, oc, w_out[l].astype(BF16), _row(norm_mix_post[l]), h,
                  _row(norm_ffn_pre[l]), ffn_up[l].astype(BF16),
                  _pad_rows(ffn_conv[l], SUBLANES), ffn_down[l].astype(BF16),
                  _row(norm_ffn_post[l]), bsz, tile)

    return h.reshape(bsz, t_pad, D_MODEL)[:, N_META:N_META + seq]
```

```python
import functools

import jax
import jax.numpy as jnp
from jax import lax
from jax.experimental import pallas as pl
from jax.experimental.pallas import tpu as pltpu

F32 = jnp.float32
BF16 = jnp.bfloat16

D_MODEL = 1024
N_META = 16
HEAD_DIM = 64
LANES = 128
SUBLANES = 8
LRU_WIDTH = 256
LRU_C = 8.0
LRU_BLOCK = 64
CONV_A = 4
FOX_WIDTH = 512
FOX_HEADS = 8
FOX_PAIRS = FOX_HEADS // 2
FOX_STEP_HEADS = 8
RWKV_WIDTH = 256
RWKV_PAIRS = RWKV_WIDTH // LANES
W_RANK = 64
A_RANK = 64
G_RANK = 128
V_RANK = 32
GN_EPS = 64e-5
LOG2_E = 1.4426950408889634
D_FF = 2816
FF_CHUNK = 256
CONV_F = 3
EPS = 1e-6
TIME_TILE = 640
RWKV_CHUNK = 64
INV_BLOCK = 16
PREP_GROUP = 5
VMEM_LIMIT = 56 * 1024 * 1024

COL_A = 0
COL_Q = COL_A + 2 * LRU_WIDTH
COL_K = COL_Q + FOX_WIDTH
COL_V = COL_K + FOX_WIDTH
COL_F = COL_V + FOX_WIDTH
COL_C = COL_F + LANES
N_PROJ = COL_C + 4 * RWKV_WIDTH
BIAS_PIECES = 3
LANE_ONE = HEAD_DIM


def _time_plan(t):
    if t <= TIME_TILE:
        t_pad = -(-t // LANES) * LANES
        return t_pad, t_pad
    return -(-t // TIME_TILE) * TIME_TILE, TIME_TILE


def _params(sem, flags=None):
    return pltpu.CompilerParams(dimension_semantics=sem, vmem_limit_bytes=VMEM_LIMIT, flags=flags)


def _rms(x, g):
    return x * lax.rsqrt(jnp.mean(x * x, axis=-1, keepdims=True) + EPS) * g


def _softplus(x):
    return jnp.maximum(x, 0.0) + jnp.log1p(jnp.exp(-jnp.abs(x)))


def _sigmoid(x):
    return 1.0 / (1.0 + jnp.exp(-x))


def _gelu_tanh(x):
    return 0.5 * x * (1.0 + jnp.tanh(0.7978845608028654 * (x + 0.044715 * x * x * x)))


def _split(x):
    hi = x.astype(BF16)
    lo = (x - hi.astype(F32)).astype(BF16)
    return hi, lo


_NN = (((1,), (0,)), ((), ()))
_NT = (((1,), (1,)), ((), ()))
_TN = (((0,), (0,)), ((), ()))


def _mm3(a, b, dims=_NN):
    ah, al = a
    bh, bl = b
    d = functools.partial(lax.dot_general, dimension_numbers=dims, preferred_element_type=F32)
    return d(ah, bh) + (d(ah, bl) + d(al, bh))


def _mmf(a, b, dims=_NN):
    return _mm3(_split(a), _split(b), dims)


def _dot(a, b, dims=_NN):
    return lax.dot_general(a, b, dimension_numbers=dims, preferred_element_type=F32)


def _scan_steps(n):
    d = 1
    while d < n:
        yield d
        d *= 2


def _inproj_body(h_ref, g_ref, w_ref, fb_ref, cw_ref, cb_ref, wa_ref, ba_ref, wx_ref, bx_ref,
                 lam_ref, oa_ref, q_ref, k_ref, v_ref, pc_ref, ccar_ref, ext_ref, hcar_ref, lin_ref,
                 *, tiles_per_seq):
    tm = h_ref.shape[0]
    seq_start = pl.program_id(0) % tiles_per_seq == 0
    xn = _rms(h_ref[...], g_ref[...]).astype(BF16)

    def proj(lo, n):
        return jnp.dot(xn, w_ref[:, lo:lo + n], preferred_element_type=F32)

    def projections():
        step = 2 * LANES
        for c0 in range(0, 4 * RWKV_WIDTH, step):
            pc_ref[:, c0:c0 + step] = proj(COL_C + c0, step)
            yield

        @pl.when(seq_start)
        def _():
            ccar_ref[...] = jnp.zeros_like(ccar_ref)

        c = -_softplus(-(proj(COL_F, LANES) + fb_ref[...]))
        row = lax.broadcasted_iota(jnp.int32, (tm, LANES), 0)
        for d in _scan_steps(tm):
            c = c + jnp.where(row >= d, pltpu.roll(c, d, axis=0), 0.0)
        c = c + ccar_ref[0:1, :]
        ccar_ref[...] = jnp.broadcast_to(c[tm - 1:tm, :], ccar_ref.shape)

        pieces = []
        rest = c * LOG2_E
        for _ in range(BIAS_PIECES):
            piece = rest.astype(BF16).astype(F32)
            pieces.append(piece)
            rest = rest - piece
        yield

        lane = lax.broadcasted_iota(jnp.int32, (tm, LANES), 1)
        spare = lane - HEAD_DIM
        in_first = (spare >= 0) & (spare < BIAS_PIECES)
        in_second = (spare >= BIAS_PIECES) & (spare < 2 * BIAS_PIECES)
        q_ones = jnp.where(in_second, 1.0, 0.0)
        k_ones = jnp.where(in_first, 1.0, 0.0)
        v_extra = jnp.where(lane == LANE_ONE, 1.0, 0.0)
        qkv = []
        for col in (COL_Q, COL_K, COL_V):
            qkv.append(proj(col, FOX_WIDTH))
            yield
        q, k, v = qkv
        for hd in range(FOX_HEADS):
            g = hd // 2

            def head_tile(x):
                x = x[:, g * LANES:(g + 1) * LANES]
                return pltpu.roll(x, HEAD_DIM, axis=1) if hd % 2 else x

            cols = [jnp.broadcast_to(p[:, hd:hd + 1], (tm, LANES)) for p in pieces]
            sel = cols[BIAS_PIECES - 1]
            for j in range(BIAS_PIECES - 1):
                sel = jnp.where(spare % BIAS_PIECES == j, cols[j], sel)
            qx = jnp.where(in_first, sel, q_ones)
            kx = jnp.where(in_second, -sel, k_ones)
            q_ref[hd] = jnp.where(lane < HEAD_DIM, head_tile(q), qx).astype(BF16)
            k_ref[hd] = jnp.where(lane < HEAD_DIM, head_tile(k), kx).astype(BF16)
            v_ref[hd] = jnp.where(lane < HEAD_DIM, head_tile(v), v_extra).astype(BF16)
            yield

    pa = proj(COL_A, 2 * LRU_WIDTH)
    streams = [_lru_tile(pa, seq_start, cw_ref, cb_ref, wa_ref, ba_ref, wx_ref, bx_ref, lam_ref,
                         ext_ref, hcar_ref, lin_ref, oa_ref),
               projections()]
    while streams:
        streams = [s for s in streams if next(s, True) is None]


def _inproj(h, g, w, fb, lru_prm, bsz, tm):
    rows = h.shape[0]
    row = lambda i: (i, 0)
    const = lambda i: (0, 0)
    head = pl.BlockSpec((FOX_HEADS, tm, LANES), lambda i: (0, i, 0))
    head_shape = jax.ShapeDtypeStruct((FOX_HEADS, rows, LANES), BF16)
    return pl.pallas_call(
        functools.partial(_inproj_body, tiles_per_seq=rows // bsz // tm),
        grid=(rows // tm,),
        in_specs=[pl.BlockSpec((tm, D_MODEL), row),
                  pl.BlockSpec((1, D_MODEL), const),
                  pl.BlockSpec((D_MODEL, N_PROJ), const),
                  pl.BlockSpec(fb.shape, const)]
                 + [pl.BlockSpec(a.shape, const) for a in lru_prm],
        out_specs=[pl.BlockSpec((tm, LRU_WIDTH), row), head, head, head,
                   pl.BlockSpec((tm, 4 * RWKV_WIDTH), row)],
        out_shape=[jax.ShapeDtypeStruct((rows, LRU_WIDTH), F32),
                   head_shape, head_shape, head_shape,
                   jax.ShapeDtypeStruct((rows, 4 * RWKV_WIDTH), F32)],
        scratch_shapes=[pltpu.VMEM((SUBLANES, LANES), F32),
                        pltpu.VMEM((tm + SUBLANES, LRU_WIDTH), F32),
                        pltpu.VMEM((SUBLANES, LRU_WIDTH), F32),
                        pltpu.VMEM((tm, 4 * LRU_WIDTH), F32)],
        compiler_params=_params(("arbitrary",)),
        name="inproj",
    )(h, g, w, fb, *lru_prm)


def _lru_tile(pa, seq_start, cw_ref, cb_ref, wa_ref, ba_ref, wx_ref, bx_ref, lam_ref,
              ext_ref, hcar_ref, lin_ref, oa_ref):
    tt = pa.shape[0]

    @pl.when(seq_start)
    def _():
        ext_ref[0:SUBLANES, :] = jnp.zeros((SUBLANES, LRU_WIDTH), F32)
        hcar_ref[...] = jnp.zeros_like(hcar_ref)

    xa = pa[:, 0:LRU_WIDTH]
    ya = pa[:, LRU_WIDTH:2 * LRU_WIDTH]
    ext_ref[SUBLANES:, :] = xa
    u = cb_ref[...]
    for k in range(CONV_A):
        off = SUBLANES - (CONV_A - 1) + k
        u = u + cw_ref[k:k + 1, :] * ext_ref[pl.ds(off, tt), :]
    ext_ref[0:SUBLANES, :] = xa[tt - SUBLANES:, :]
    ub = u.astype(BF16)
    lin_ref[:, 0:LRU_WIDTH] = jnp.dot(ub, wa_ref[...], preferred_element_type=F32) + ba_ref[...]
    lin_ref[:, LRU_WIDTH:2 * LRU_WIDTH] = (jnp.dot(ub, wx_ref[...], preferred_element_type=F32)
                                           + bx_ref[...])
    lin_ref[:, 2 * LRU_WIDTH:3 * LRU_WIDTH] = u
    lin_ref[:, 3 * LRU_WIDTH:4 * LRU_WIDTH] = ya
    yield

    neg_rate = (-LRU_C) * _softplus(-lam_ref[...])
    row = lax.broadcasted_iota(jnp.int32, (LRU_BLOCK, LRU_WIDTH), 0)
    carry = hcar_ref[0:1, :]
    for r0 in range(0, tt, LRU_BLOCK):
        rows = slice(r0, r0 + LRU_BLOCK)
        r = _sigmoid(lin_ref[rows, 0:LRU_WIDTH])
        ig = _sigmoid(lin_ref[rows, LRU_WIDTH:2 * LRU_WIDTH])
        log_a = neg_rate * r
        a = jnp.exp(log_a)
        b = (jnp.sqrt(-jnp.tanh(log_a) * (a * a + 1.0))
             * (ig * lin_ref[rows, 2 * LRU_WIDTH:3 * LRU_WIDTH]))
        for d in _scan_steps(LRU_BLOCK):
            live = row >= d
            b = b + a * jnp.where(live, pltpu.roll(b, d, axis=0), 0.0)
            a = a * jnp.where(live, pltpu.roll(a, d, axis=0), 1.0)
        h = b + a * carry
        carry = h[LRU_BLOCK - 1:LRU_BLOCK, :]
        oa_ref[rows, :] = h * _gelu_tanh(lin_ref[rows, 3 * LRU_WIDTH:4 * LRU_WIDTH])
        yield
    hcar_ref[...] = jnp.broadcast_to(carry, hcar_ref.shape)


def _fox_body(qi_ref, ki_ref, q_ref, k_ref, v_ref, o_ref,
              m_ref, acc_ref, mask_ref, s_ref, mb_ref):
    p = pl.program_id(2)
    qi = qi_ref[p]
    ki = ki_ref[p]
    tq = q_ref.shape[1]
    tk = k_ref.shape[1]

    @pl.when((pl.program_id(0) == 0) & (pl.program_id(1) == 0) & (p == 0))
    def _():
        causal = (lax.broadcasted_iota(jnp.int32, (tq, tk), 1)
                  <= lax.broadcasted_iota(jnp.int32, (tq, tk), 0))
        mask_ref[...] = jnp.where(causal, 0.0, -jnp.inf)

    @pl.when(ki == qi)
    def _():
        m_ref[...] = jnp.full_like(m_ref, -jnp.inf)
        acc_ref[...] = jnp.zeros_like(acc_ref)

    def update(masked):
        def scores(e):
            s = lax.dot_general(q_ref[e], k_ref[e], _NT, preferred_element_type=F32)
            if masked:
                s = s + mask_ref[...]
            s_ref[e] = s
            mx = s[:, 0:LANES]
            for c0 in range(LANES, tk, LANES):
                mx = jnp.maximum(mx, s[:, c0:c0 + LANES])
            mb_ref[e] = mx

        for e in range(FOX_STEP_HEADS):
            scores(e)
        for e in range(FOX_STEP_HEADS):
            m_prev = m_ref[e]
            m_new = jnp.maximum(m_prev, jnp.max(mb_ref[e], axis=1, keepdims=True))
            m_ref[e] = m_new
            mb_ref[e] = jnp.broadcast_to(m_new, (tq, LANES))
            acc_ref[e] = jnp.exp2(m_prev - m_new) * acc_ref[e]
            shift = jnp.concatenate([mb_ref[e]] * (tk // LANES), axis=1)
            pr = jnp.exp2(s_ref[e] - shift).astype(BF16)
            acc_ref[e] += jnp.dot(pr, v_ref[e], preferred_element_type=F32)

    @pl.when(ki < qi)
    def _():
        update(False)

    @pl.when(ki == qi)
    def _():
        update(True)

    @pl.when(ki == 0)
    def _():
        lane = lax.broadcasted_iota(jnp.int32, (tq, LANES), 1)
        for g in range(FOX_STEP_HEADS // 2):
            o0, o1 = (acc_ref[e] / acc_ref[e][:, LANE_ONE:LANE_ONE + 1] for e in (2 * g, 2 * g + 1))
            o_ref[g] = jnp.where(lane < HEAD_DIM, o0,
                                 pltpu.roll(o1, HEAD_DIM, axis=1)).astype(BF16)


def _fox(qe, ke, ve, bsz, tq):
    rows = qe.shape[1]
    nq = rows // bsz // tq
    nh = FOX_STEP_HEADS
    pairs = [(i, j) for i in range(nq) for j in reversed(range(i + 1))]
    qi_tab = jnp.asarray([p[0] for p in pairs], jnp.int32)
    ki_tab = jnp.asarray([p[1] for p in pairs], jnp.int32)
    grid_spec = pltpu.PrefetchScalarGridSpec(
        num_scalar_prefetch=2,
        grid=(bsz, FOX_HEADS // nh, len(pairs)),
        in_specs=[
            pl.BlockSpec((nh, tq, LANES), lambda b, g, p, qi, ki: (g, b * nq + qi[p], 0)),
            pl.BlockSpec((nh, tq, LANES), lambda b, g, p, qi, ki: (g, b * nq + ki[p], 0)),
            pl.BlockSpec((nh, tq, LANES), lambda b, g, p, qi, ki: (g, b * nq + ki[p], 0)),
        ],
        out_specs=pl.BlockSpec((nh // 2, tq, LANES),
                               lambda b, g, p, qi, ki: (g, b * nq + qi[p], 0)),
        scratch_shapes=[pltpu.VMEM((nh, tq, 1), F32), pltpu.VMEM((nh, tq, LANES), F32),
                        pltpu.VMEM((tq, tq), F32), pltpu.VMEM((nh, tq, tq), F32),
                        pltpu.VMEM((nh, tq, LANES), F32)],
    )
    return pl.pallas_call(
        _fox_body,
        grid_spec=grid_spec,
        out_shape=jax.ShapeDtypeStruct((FOX_PAIRS, rows, LANES), BF16),
        compiler_params=_params(("arbitrary", "arbitrary", "arbitrary"),
                                flags=None),
        name="fox",
    )(qi_tab, ki_tab, qe, ke, ve)


def _head_sum(x, ones_bd):
    hi, lo = _split(x)
    d = functools.partial(jnp.dot, preferred_element_type=F32)
    return d(hi, ones_bd) + d(lo, ones_bd)


def _block_diag2(y):
    zero = jnp.zeros((y.shape[0], LANES), y.dtype)
    return jnp.concatenate([jnp.concatenate([y[:, :LANES], zero], axis=1),
                            jnp.concatenate([zero, y[:, LANES:]], axis=1)], axis=0)


def _twin(x, y, dims=_NN):
    return _dot(x, _block_diag2(y), dims)


def _twin_tn(x, y):
    stacked = jnp.concatenate([x[:, :LANES], x[:, LANES:]], axis=0)
    return _dot(stacked, _block_diag2(y), _TN)


def _unit_lower_inverses(mats, eye, out):
    r = lax.broadcasted_iota(jnp.int32, eye.shape, 0) // INV_BLOCK
    c = (lax.broadcasted_iota(jnp.int32, eye.shape, 1) % LANES) // INV_BLOCK
    dgs = [jnp.where(r == c, a, 0.0) for a in mats]
    offs = [(a - dg).astype(BF16) for a, dg in zip(mats, dgs)]
    pws = [dg.astype(BF16) for dg in dgs]
    accs = [eye - dg for dg in dgs]
    for _ in range(3):
        pws = [_twin(pw, pw).astype(BF16) for pw in pws]
        yield
        accs = [acc + _twin(acc.astype(BF16), pw) for acc, pw in zip(accs, pws)]
        yield
    dinvs = [acc.astype(BF16) for acc in accs]
    nns = [_twin(dinv, off) for dinv, off in zip(dinvs, offs)]
    yield
    sqs = [_twin(nn.astype(BF16), nn.astype(BF16)).astype(BF16) for nn in nns]
    yield
    m2s = [(eye - nn) + _twin((eye - nn).astype(BF16), sq) for nn, sq in zip(nns, sqs)]
    yield
    out.extend(_twin(m2.astype(BF16), dinv) for m2, dinv in zip(m2s, dinvs))


def _rwkv_body(*refs, has_vmix):
    it = iter(refs)
    pc_ref = next(it)
    vf_ref = next(it) if has_vmix else None
    (mu_ref, w0_ref, wup_ref, a0_ref, aup_ref, gup_ref, kk_ref, ka_ref, rk_ref,
     lnw_ref, lnb_ref) = (next(it) for _ in range(11))
    if has_vmix:
        v0_ref, vdn_ref, vup_ref = (next(it) for _ in range(3))
    oc_ref = next(it)
    vown_ref = None if has_vmix else next(it)
    prev_ref, h_ref, gc_ref, o_ref = (next(it) for _ in range(4))
    kks_ref, rs_ref, bts_ref, kts_ref, bhs_ref, khs_ref, vs_ref = (next(it) for _ in range(7))
    tinv_ref, arb_ref, avx_ref, akv_ref, khv_ref, gful_ref = (next(it) for _ in range(6))

    tt = pc_ref.shape[0]
    nch = tt // RWKV_CHUNK
    width = RWKV_WIDTH

    @pl.when(pl.program_id(1) == 0)
    def _():
        prev_ref[...] = jnp.zeros_like(prev_ref)
        h_ref[...] = jnp.zeros_like(h_ref)

    pc = pc_ref[...]
    row = lax.broadcasted_iota(jnp.int32, pc.shape, 0)
    shifted = jnp.where(row == 0, prev_ref[SUBLANES - 1:SUBLANES, :], pltpu.roll(pc, 1, axis=0))
    prev_ref[...] = pc[tt - SUBLANES:, :]
    pc = pc + (shifted - pc) * mu_ref[...]
    r = pc[:, 0:width]
    k = pc[:, width:2 * width]
    v = pc[:, 2 * width:3 * width]
    wa_dn = pc[:, 3 * width:3 * width + LANES]
    g_dn = pc[:, 3 * width + LANES:3 * width + 2 * LANES]

    rr = lax.broadcasted_iota(jnp.int32, (width, width), 0) // HEAD_DIM
    cc = lax.broadcasted_iota(jnp.int32, (width, width), 1) // HEAD_DIM
    ones_bd = jnp.where(rr == cc, 1.0, 0.0).astype(BF16)

    if has_vmix:
        vm = _mmf(_mmf(v, vdn_ref[...]), vup_ref[...])
        v = v + (vf_ref[...] - v) * _sigmoid(v0_ref[...] + vm)
    else:
        vown_ref[...] = v
    w_log = -_softplus(-(w0_ref[...] + _mmf(jnp.tanh(wa_dn), wup_ref[...]))) - 0.5
    lw = -jnp.exp(w_log)
    a = _sigmoid(a0_ref[...] + _mmf(wa_dn, aup_ref[...]))
    gate = _mmf(_sigmoid(g_dn), gup_ref[...])
    kkr = k * kk_ref[...]
    kk = kkr / jnp.maximum(jnp.sqrt(_head_sum(kkr * kkr, ones_bd)), 1e-12)
    k = k * (1.0 + (a - 1.0) * ka_ref[...])
    b = a * kk
    bonus = _head_sum(r * k * rk_ref[...], ones_bd) * v

    cum = lw
    rowc = lax.broadcasted_iota(jnp.int32, (tt, width), 0) % RWKV_CHUNK
    for d in _scan_steps(RWKV_CHUNK):
        cum = cum + jnp.where(rowc >= d, pltpu.roll(cum, d, axis=0), 0.0)
    cum3 = cum.reshape(nch, RWKV_CHUNK, width)
    clast = jnp.broadcast_to(cum3[:, RWKV_CHUNK - 1:RWKV_CHUNK, :], cum3.shape).reshape(tt, width)
    ginv = jnp.exp(-cum)
    ghat = jnp.exp(clast - cum)
    gc_ref[...] = jnp.exp(clast)

    n2 = 2 * RWKV_CHUNK
    lane3 = lax.broadcasted_iota(jnp.int32, (nch, RWKV_CHUNK, width), 2) % LANES

    def stack(ref, x):
        xp = x.reshape(nch, RWKV_CHUNK, width)
        st = jnp.concatenate([jnp.where(lane3 < HEAD_DIM, xp, 0.0),
                              jnp.where(lane3 < HEAD_DIM, 0.0, xp)], axis=1)
        ref[...] = st.reshape(nch * n2, width).astype(BF16)

    stack(kks_ref, kk * jnp.exp(cum - lw))
    stack(rs_ref, r * jnp.exp(cum))
    stack(bts_ref, b * ginv)
    stack(kts_ref, k * ginv)
    stack(bhs_ref, b * ghat)
    stack(khs_ref, k * ghat)
    stack(vs_ref, v)

    ri = lax.broadcasted_iota(jnp.int32, (n2, width), 0)
    ci = lax.broadcasted_iota(jnp.int32, (n2, width), 1) % LANES
    eye = jnp.where(ri == ci, 1.0, 0.0)

    group = max(d for d in range(1, PREP_GROUP + 1) if nch % d == 0)

    def prepare(it):
        probs = [(slice((it * group + j) * n2, (it * group + j + 1) * n2),
                  (it * group + j) * RWKV_CHUNK) for j in range(group)]
        kks = [kks_ref[rows, :] for rows, _ in probs]
        rs = [rs_ref[rows, :] for rows, _ in probs]
        bts = [bts_ref[rows, :] for rows, _ in probs]
        kts = [kts_ref[rows, :] for rows, _ in probs]
        quads = [_dot(jnp.concatenate([x, z], axis=0),
                      jnp.concatenate([_block_diag2(y), _block_diag2(w)], axis=0), _NT)
                 for x, z, y, w in zip(kks, rs, bts, kts)]
        a_ab = [jnp.where(ri > ci, q[0:n2, 0:width], 0.0) for q in quads]
        yield
        tinvs = []
        yield from _unit_lower_inverses(a_ab, eye, tinvs)
        for (rows, _), t in zip(probs, tinvs):
            tinv_ref[rows, :] = t.astype(BF16)
        yield
        for (rows, _), q in zip(probs, quads):
            arb_ref[rows, :] = jnp.where(ri >= ci, q[n2:2 * n2, 0:width], 0.0).astype(BF16)
        yield
        for (rows, _), q in zip(probs, quads):
            a_k = jnp.concatenate([jnp.where(ri > ci, q[0:n2, width:2 * width], 0.0),
                                   jnp.where(ri >= ci, q[n2:2 * n2, width:2 * width], 0.0)],
                                  axis=0).astype(BF16)
            av = _twin(a_k, vs_ref[rows, :])
            avx_ref[rows, :] = av[0:n2]
            akv_ref[rows, :] = av[n2:2 * n2]
        yield
        for rows, r0 in probs:
            khv_ref[rows, :] = _twin_tn(khs_ref[rows, :], vs_ref[rows, :])
            diag = jnp.where(ri == ci, jnp.broadcast_to(gc_ref[pl.ds(r0, 1), :], (n2, width)), 0.0)
            gful_ref[rows, :] = jnp.concatenate(
                [jnp.broadcast_to(jnp.sum(diag[:, g * LANES:(g + 1) * LANES], axis=1,
                                          keepdims=True), (n2, LANES))
                 for g in range(RWKV_PAIRS)], axis=1)

    def advance(ch):
        rows = slice(ch * n2, (ch + 1) * n2)
        r0 = ch * RWKV_CHUNK
        tiles = [slice(g * LANES, (g + 1) * LANES) for g in range(RWKV_PAIRS)]
        hs = [h_ref[:, t] for t in tiles]
        hbs = [h.astype(BF16) for h in hs]
        khs = [_dot(jnp.concatenate([kks_ref[rows, t], rs_ref[rows, t]], axis=0), hb)
               for t, hb in zip(tiles, hbs)]
        xs = [kh[0:n2] + avx_ref[rows, t] for t, kh in zip(tiles, khs)]
        yield
        us = [(-_dot(tinv_ref[rows, t], x.astype(BF16))).astype(BF16) for t, x in zip(tiles, xs)]
        yield
        for t, h, u in zip(tiles, hs, us):
            h_ref[:, t] = h * gful_ref[rows, t] + _dot(bhs_ref[rows, t], u, _TN) + khv_ref[rows, t]
        for t, kh, u in zip(tiles, khs, us):
            o = kh[n2:2 * n2] + _dot(arb_ref[rows, t], u) + akv_ref[rows, t]
            o_ref[pl.ds(r0, RWKV_CHUNK), t] = o[0:RWKV_CHUNK] + o[RWKV_CHUNK:]
        yield

    def advance_group(it):
        for j in range(group):
            yield from advance(it * group + j)

    for _ in prepare(0):
        pass
    for it in range(nch // group):
        streams = [advance_group(it)]
        if it + 1 < nch // group:
            streams.append(prepare(it + 1))
        while streams:
            streams = [s for s in streams if next(s, True) is None]

    o = o_ref[...]
    mean = _head_sum(o, ones_bd) * (1.0 / HEAD_DIM)
    dev = o - mean
    var = _head_sum(dev * dev, ones_bd) * (1.0 / HEAD_DIM)
    o = dev * lax.rsqrt(var + GN_EPS) * lnw_ref[...] + lnb_ref[...]
    oc_ref[...] = (o + bonus) * gate


def _rwkv(pc, vfirst, prm, vmix_prm, bsz, tt):
    rows = pc.shape[0]
    nt = rows // bsz // tt
    row = lambda b, i: (b * nt + i, 0)
    const = lambda b, i: (0, 0)
    full = lambda a: pl.BlockSpec(a.shape, const)
    has_vmix = vfirst is not None
    inputs = [pc] + ([vfirst] if has_vmix else []) + list(prm) + (list(vmix_prm) if has_vmix else [])
    in_specs = ([pl.BlockSpec((tt, 4 * RWKV_WIDTH), row)]
                + ([pl.BlockSpec((tt, RWKV_WIDTH), row)] if has_vmix else [])
                + [full(a) for a in prm] + ([full(a) for a in vmix_prm] if has_vmix else []))
    n_out = 1 if has_vmix else 2
    tile = pltpu.VMEM((tt, RWKV_WIDTH), F32)
    stacked = lambda dt: pltpu.VMEM((2 * tt, RWKV_WIDTH), dt)
    outs = pl.pallas_call(
        functools.partial(_rwkv_body, has_vmix=has_vmix),
        grid=(bsz, nt),
        in_specs=in_specs,
        out_specs=[pl.BlockSpec((tt, RWKV_WIDTH), row)] * n_out,
        out_shape=[jax.ShapeDtypeStruct((rows, RWKV_WIDTH), F32)] * n_out,
        scratch_shapes=[pltpu.VMEM((SUBLANES, 4 * RWKV_WIDTH), F32),
                        pltpu.VMEM((2 * RWKV_CHUNK, RWKV_WIDTH), F32), tile, tile]
                       + [stacked(BF16)] * 9 + [stacked(F32)] * 4,
        compiler_params=_params(("arbitrary", "arbitrary")),
        name="rwkv_vmix" if has_vmix else "rwkv",
    )(*inputs)
    return (outs[0], None) if has_vmix else (outs[0], outs[1])


def _tail_body(oa_ref, ob_ref, oc_ref, wout_ref, gmix_ref, h_ref, gpre_ref, wup_ref, conv_ref,
               wdn_ref, gpost_ref, out_ref, h1_ref, xn_ref, u_ref, act_ref, *, tiles_per_seq):
    i = pl.program_id(0)
    tm = h_ref.shape[0]
    fc = FF_CHUNK

    m = jnp.concatenate([oa_ref[...].astype(BF16)] + [ob_ref[g] for g in range(FOX_PAIRS)]
                        + [oc_ref[...].astype(BF16)], axis=1)
    mixed = jnp.dot(m, wout_ref[...], preferred_element_type=F32)
    h1_ref[...] = h_ref[...] + _rms(mixed, gmix_ref[...])

    @pl.when(i % tiles_per_seq == 0)
    def _():
        xn_ref[0:SUBLANES, :] = jnp.zeros((SUBLANES, D_MODEL), BF16)

    @pl.when(i % tiles_per_seq != 0)
    def _():
        xn_ref[0:SUBLANES, :] = xn_ref[tm:tm + SUBLANES, :]

    xn_ref[SUBLANES:, :] = _rms(h1_ref[...], gpre_ref[...]).astype(BF16)
    xn = xn_ref[...]
    for j in range(D_FF // fc):
        gcols = slice(j * fc, (j + 1) * fc)
        vcols = slice(D_FF + j * fc, D_FF + (j + 1) * fc)
        u = u_ref.at[j % 2]
        u[:, 0:fc] = jnp.dot(xn, wup_ref[:, gcols], preferred_element_type=F32)
        u[:, fc:2 * fc] = jnp.dot(xn, wup_ref[:, vcols], preferred_element_type=F32)
        fg = jnp.zeros((tm, fc), F32)
        fv = jnp.zeros((tm, fc), F32)
        for k in range(CONV_F):
            off = SUBLANES - (CONV_F - 1) + k
            fg = fg + conv_ref[k:k + 1, gcols] * u[pl.ds(off, tm), 0:fc]
            fv = fv + conv_ref[k:k + 1, vcols] * u[pl.ds(off, tm), fc:2 * fc]
        act_ref[:, gcols] = (_gelu_tanh(fg) * fv).astype(BF16)
    y = jnp.dot(act_ref[...], wdn_ref[...], preferred_element_type=F32)
    out_ref[...] = h1_ref[...] + _rms(y, gpost_ref[...])


def _tail(oa, ob, oc, wout, gmix, h, gpre, wup, conv, wdn, gpost, bsz, tm):
    rows = h.shape[0]
    row = lambda i: (i, 0)
    const = lambda i: (0, 0)
    resident = lambda a: pl.BlockSpec(a.shape, const, pipeline_mode=pl.Buffered(1))
    gain = pl.BlockSpec((1, D_MODEL), const)
    return pl.pallas_call(
        functools.partial(_tail_body, tiles_per_seq=rows // bsz // tm),
        grid=(rows // tm,),
        in_specs=[pl.BlockSpec((tm, LRU_WIDTH), row),
                  pl.BlockSpec((FOX_PAIRS, tm, LANES), lambda i: (0, i, 0)),
                  pl.BlockSpec((tm, RWKV_WIDTH), row),
                  resident(wout), gain,
                  pl.BlockSpec((tm, D_MODEL), row), gain,
                  resident(wup), resident(conv), resident(wdn), gain],
        out_specs=pl.BlockSpec((tm, D_MODEL), row),
        out_shape=jax.ShapeDtypeStruct((rows, D_MODEL), F32),
        scratch_shapes=[pltpu.VMEM((tm, D_MODEL), F32),
                        pltpu.VMEM((tm + SUBLANES, D_MODEL), BF16),
                        pltpu.VMEM((2, tm + SUBLANES, 2 * FF_CHUNK), F32),
                        pltpu.VMEM((tm, D_FF), BF16)],
        input_output_aliases={5: 0},
        compiler_params=_params(("arbitrary",)),
        name="tail",
    )(oa, ob, oc, wout, gmix, h, gpre, wup, conv, wdn, gpost)


def _row(x):
    return x.reshape(1, -1).astype(F32)


def _pad_rows(x, n):
    return jnp.pad(x, ((0, n - x.shape[0]), (0, 0)))


def _block_diag(w):
    heads, d, _ = w.shape
    eye = jnp.eye(heads, dtype=w.dtype)
    return (w[:, :, None, :] * eye[:, None, :, None]).reshape(heads * d, heads * d)


def _pack_w_in(w):
    wa = w[:, :2 * LRU_WIDTH]
    o = 2 * LRU_WIDTH
    wq = w[:, o:o + FOX_WIDTH] * (HEAD_DIM ** -0.5 * LOG2_E)
    wk = w[:, o + FOX_WIDTH:o + 2 * FOX_WIDTH]
    wv = w[:, o + 2 * FOX_WIDTH:o + 3 * FOX_WIDTH]
    wf = w[:, o + 3 * FOX_WIDTH:o + 3 * FOX_WIDTH + FOX_HEADS]
    wc = w[:, o + 3 * FOX_WIDTH + FOX_HEADS:]
    wf = jnp.pad(wf, ((0, 0), (0, LANES - FOX_HEADS)))
    return jnp.concatenate([wa, wq, wk, wv, wf, wc], axis=1).astype(BF16)


def kernel(x, meta_tokens, norm_mix_pre, norm_mix_post, norm_ffn_pre, norm_ffn_post, w_in, w_out, lru_conv_w, lru_conv_b, lru_gate_a_w, lru_gate_a_b, lru_gate_x_w, lru_gate_x_b, lru_lambda, fox_f_bias, rwkv_mu, rwkv_w0, rwkv_w_up, rwkv_a0, rwkv_a_up, rwkv_g_up, rwkv_v0, rwkv_v_down, rwkv_v_up, rwkv_k_k, rwkv_k_a, rwkv_r_k, rwkv_ln_w, rwkv_ln_b, ffn_up, ffn_conv, ffn_down):
    bsz, seq, _ = x.shape
    depth = w_in.shape[0]
    t = N_META + seq
    t_pad, tile = _time_plan(t)
    rows = bsz * t_pad

    meta = jnp.broadcast_to(meta_tokens.astype(x.dtype)[None], (bsz, N_META, D_MODEL))
    h = jnp.concatenate([meta, x, jnp.zeros((bsz, t_pad - t, D_MODEL), x.dtype)], axis=1)
    h = h.reshape(rows, D_MODEL)

    v_first = None
    for l in range(depth):
        f_bias = jnp.pad(_row(fox_f_bias[l]), ((0, 0), (0, LANES - FOX_HEADS)))
        lru_prm = (_pad_rows(lru_conv_w[l], SUBLANES), _row(lru_conv_b[l]),
                   _block_diag(lru_gate_a_w[l]).astype(BF16), _row(lru_gate_a_b[l]),
                   _block_diag(lru_gate_x_w[l]).astype(BF16), _row(lru_gate_x_b[l]),
                   _row(lru_lambda[l]))
        oa, qe, ke, ve, pc = _inproj(h, _row(norm_mix_pre[l]), _pack_w_in(w_in[l]), f_bias,
                                     lru_prm, bsz, tile)

        ob = _fox(qe, ke, ve, bsz, tile)

        rwkv_prm = (_row(rwkv_mu[l]), _row(rwkv_w0[l]),
                    _pad_rows(rwkv_w_up[l], LANES), _row(rwkv_a0[l]),
                    jnp.pad(rwkv_a_up[l], ((W_RANK, 0), (0, 0))), rwkv_g_up[l],
                    _row(rwkv_k_k[l]), _row(rwkv_k_a[l]), _row(rwkv_r_k[l]),
                    _row(rwkv_ln_w[l]), _row(rwkv_ln_b[l]))
        vmix_prm = None
        if l > 0:
            vmix_prm = (_row(rwkv_v0[l - 1]),
                        jnp.pad(rwkv_v_down[l - 1], ((0, 0), (0, LANES - V_RANK))),
                        _pad_rows(rwkv_v_up[l - 1], LANES))
        oc, v_own = _rwkv(pc, v_first if l > 0 else None, rwkv_prm, vmix_prm, bsz, tile)
        if l == 0:
            v_first = v_own

        h = _tail(oa, ob, oc, w_out[l].astype(BF16), _row(norm_mix_post[l]), h,
                  _row(norm_ffn_pre[l]), ffn_up[l].astype(BF16),
                  _pad_rows(ffn_conv[l], SUBLANES), ffn_down[l].astype(BF16),
                  _row(norm_ffn_post[l]), bsz, tile)

    return h.reshape(bsz, t_pad, D_MODEL)[:, N_META:N_META + seq]
```
